```python
import jax, jax.numpy as jnp
from jax import lax
import numpy as np

D_MODEL = 1024
BATCH = 8
SEQ = 4096
DEPTH = 2

RMS_EPS = 1e-6
L2_EPS = 1e-6
NEG_INF = -1e30

DN_HEADS = 4
DN_HEAD_DIM = 128
DN_WIDTH = DN_HEADS * DN_HEAD_DIM
DN_CONV = 4
DN_CHUNK = 64

SC_GROUPS = 4
SC_WIDTH = SC_GROUPS * 128
SC_CONV = 3

SWA_Q_HEADS = 16
SWA_KV_HEADS = 2
SWA_HEAD_DIM = 64
SWA_WINDOW = 128
SWA_BLOCK = 128

FFN_DIM = 3584
N_EXPERTS = 8
TOP_K = 2
EXPERT_DIM = 3584

EVEN_IN_SIZES = (DN_WIDTH, DN_WIDTH, DN_WIDTH,
                 DN_WIDTH,
                 DN_HEADS, DN_HEADS,
                 SC_WIDTH, SC_WIDTH, SC_WIDTH)
EVEN_IN = sum(EVEN_IN_SIZES)
EVEN_MIX = DN_WIDTH + SC_WIDTH
ODD_IN_SIZES = (SWA_Q_HEADS * SWA_HEAD_DIM, SWA_KV_HEADS * SWA_HEAD_DIM, SWA_KV_HEADS * SWA_HEAD_DIM)
ODD_IN = sum(ODD_IN_SIZES)
ODD_MIX = SWA_Q_HEADS * SWA_HEAD_DIM

kernel_name = "hybrid_deltanet_shortconv_swa_moe"


def rms_norm(x, w):
    xf = x.astype(jnp.float32)
    y = xf * lax.rsqrt(jnp.mean(xf * xf, axis=-1, keepdims=True) + RMS_EPS)
    return (y * w.astype(jnp.float32)).astype(x.dtype)


def l2_norm(x):
    xf = x.astype(jnp.float32)
    return xf * lax.rsqrt(jnp.sum(xf * xf, axis=-1, keepdims=True) + L2_EPS)


def split_cols(t, sizes):
    offs = np.cumsum(sizes)[:-1].tolist()
    return jnp.split(t, offs, axis=-1)


def causal_depthwise_conv(x, w):
    width, ch = w.shape
    return lax.conv_general_dilated(
        x, w[:, None, :].astype(x.dtype), window_strides=(1,),
        padding=[(width - 1, 0)], dimension_numbers=('NWC', 'WIO', 'NWC'),
        feature_group_count=ch)


def gated_delta_rule(q, k, v, g, beta):
    bsz, seqlen, nh, dk = q.shape
    dv = v.shape[-1]
    c = DN_CHUNK
    nc = seqlen // c

    def chunks(t):
        t = t.astype(jnp.float32).reshape((bsz, nc, c, nh) + t.shape[3:])
        return jnp.moveaxis(t, 3, 1)

    q, k, v, g, beta = (chunks(t) for t in (q, k, v, g, beta))
    q = q * (dk ** -0.5)
    g_cum = jnp.cumsum(g, axis=-1)
    causal = jnp.tril(jnp.ones((c, c), dtype=bool))
    strict = jnp.tril(jnp.ones((c, c), dtype=bool), k=-1)
    diff = g_cum[..., :, None] - g_cum[..., None, :]
    decay = jnp.where(causal, jnp.exp(jnp.where(causal, diff, 0.0)), 0.0)
    k_beta = k * beta[..., None]
    v_beta = v * beta[..., None]
    lower = jnp.where(strict, jnp.einsum('bhnid,bhnjd->bhnij', k_beta, k) * decay, 0.0)
    eye = jnp.broadcast_to(jnp.eye(c, dtype=jnp.float32), lower.shape)
    t_inv = lax.linalg.triangular_solve(lower, eye, left_side=True, lower=True,
                                        unit_diagonal=True)
    u = jnp.einsum('bhnij,bhnjd->bhnid', t_inv, v_beta)
    w = jnp.einsum('bhnij,bhnjd->bhnid', t_inv, k_beta * jnp.exp(g_cum)[..., None])
    q_dec = q * jnp.exp(g_cum)[..., None]
    attn = jnp.einsum('bhnid,bhnjd->bhnij', q, k) * decay
    k_dec = k * jnp.exp(g_cum[..., -1:] - g_cum)[..., None]
    g_end = jnp.exp(g_cum[..., -1])

    def step(state, xs):
        q_n, w_n, u_n, a_n, kd_n, ge_n = xs
        v_new = u_n - jnp.einsum('bhcd,bhde->bhce', w_n, state)
        o_n = (jnp.einsum('bhcd,bhde->bhce', q_n, state)
               + jnp.einsum('bhij,bhje->bhie', a_n, v_new))
        state = state * ge_n[..., None, None] + jnp.einsum('bhcd,bhce->bhde', kd_n, v_new)
        return state, o_n

    xs = tuple(jnp.moveaxis(t, 2, 0) for t in (q_dec, w, u, attn, k_dec, g_end))
    s0 = jnp.zeros((bsz, nh, dk, dv), jnp.float32)
    _, o = lax.scan(step, s0, xs)
    return jnp.transpose(o, (1, 0, 3, 2, 4)).reshape(bsz, seqlen, nh, dv)


def even_mixer(h, w_in, conv_qkv, a_log, dt_bias, o_norm, conv_sc, w_out):
    bsz, seqlen, _ = h.shape
    proj = jnp.einsum('bsd,de->bse', h, w_in)
    q, k, v, z, a, b, bg, cg, xin = split_cols(proj, EVEN_IN_SIZES)
    qkv = jax.nn.silu(causal_depthwise_conv(jnp.concatenate([q, k, v], axis=-1), conv_qkv))
    q, k, v = jnp.split(qkv, 3, axis=-1)
    hs = (bsz, seqlen, DN_HEADS, DN_HEAD_DIM)
    q = l2_norm(q.reshape(hs))
    k = l2_norm(k.reshape(hs))
    v = v.reshape(hs)
    beta = jax.nn.sigmoid(b.astype(jnp.float32))
    g = -jnp.exp(a_log.astype(jnp.float32)) * jax.nn.softplus(
        a.astype(jnp.float32) + dt_bias.astype(jnp.float32))
    o = gated_delta_rule(q, k, v, g, beta)
    o = rms_norm(o, o_norm) * jax.nn.silu(z.astype(jnp.float32).reshape(hs))
    o = o.reshape(bsz, seqlen, DN_WIDTH).astype(h.dtype)
    y_sc = bg * causal_depthwise_conv(cg * xin, conv_sc)
    mix = jnp.concatenate([o, y_sc], axis=-1)
    return jnp.einsum('bse,ed->bsd', mix, w_out)


def sliding_window_attention(q, k, v, sinks):
    bsz, seqlen, nq, hd = q.shape
    nkv = k.shape[2]
    grp = nq // nkv
    blk = SWA_BLOCK
    nb = seqlen // blk
    qb = q.reshape(bsz, nb, blk, nkv, grp, hd)
    pad = ((0, 0), (blk, 0), (0, 0), (0, 0))
    kp = jnp.pad(k, pad).reshape(bsz, nb + 1, blk, nkv, hd)
    vp = jnp.pad(v, pad).reshape(bsz, nb + 1, blk, nkv, hd)
    kw = jnp.concatenate([kp[:, :-1], kp[:, 1:]], axis=2)
    vw = jnp.concatenate([vp[:, :-1], vp[:, 1:]], axis=2)
    sink = sinks.astype(jnp.float32).reshape(nkv, grp)[None, :, :, None, None]
    qpos = jnp.arange(blk)[:, None] + blk
    kpos = jnp.arange(2 * blk)[None, :]
    rel = qpos - kpos
    band = (rel >= 0) & (rel < SWA_WINDOW)
    scale = hd ** -0.5

    def block(args):
        n, qn, kn, vn = args
        s = jnp.einsum('bqhgd,bkhd->bhgqk', qn, kn).astype(jnp.float32) * scale
        valid = band & (kpos + (n - 1) * blk >= 0)
        s = jnp.where(valid, s, NEG_INF)
        m = jnp.maximum(jnp.max(s, axis=-1, keepdims=True), sink)
        p = jnp.exp(s - m)
        denom = jnp.sum(p, axis=-1, keepdims=True) + jnp.exp(sink - m)
        pr = (p / denom).astype(vn.dtype)
        return jnp.einsum('bhgqk,bkhd->bqhgd', pr, vn)

    o = lax.map(block, (jnp.arange(nb), jnp.moveaxis(qb, 1, 0),
                        jnp.moveaxis(kw, 1, 0), jnp.moveaxis(vw, 1, 0)))
    return jnp.moveaxis(o, 0, 1).reshape(bsz, seqlen, nq, hd)


def odd_mixer(h, w_in, q_norm, k_norm, sinks, w_out):
    bsz, seqlen, _ = h.shape
    proj = jnp.einsum('bsd,de->bse', h, w_in)
    q, k, v = split_cols(proj, ODD_IN_SIZES)
    q = rms_norm(q.reshape(bsz, seqlen, SWA_Q_HEADS, SWA_HEAD_DIM), q_norm)
    k = rms_norm(k.reshape(bsz, seqlen, SWA_KV_HEADS, SWA_HEAD_DIM), k_norm)
    v = v.reshape(bsz, seqlen, SWA_KV_HEADS, SWA_HEAD_DIM)
    o = sliding_window_attention(q, k, v, sinks).reshape(bsz, seqlen, ODD_MIX)
    return jnp.einsum('bse,ed->bsd', o, w_out)


def swiglu(h, w1, w3, w2):
    a = jax.nn.silu(jnp.einsum('bsd,df->bsf', h, w1)) * jnp.einsum('bsd,df->bsf', h, w3)
    return jnp.einsum('bsf,fd->bsd', a, w2)


def moe_swiglu(h, w_router, w1, w3, w2):
    logits = jnp.einsum('bsd,de->bse', h, w_router).astype(jnp.float32)
    top_val, top_idx = lax.top_k(logits, TOP_K)
    top_gate = jax.nn.softmax(top_val, axis=-1)
    gates = jnp.sum(jax.nn.one_hot(top_idx, N_EXPERTS, dtype=jnp.float32)
                    * top_gate[..., None], axis=-2).astype(h.dtype)
    y = jnp.zeros_like(h)
    for e in range(N_EXPERTS):
        y = y + gates[..., e:e + 1] * swiglu(h, w1[e], w3[e], w2[e])
    return y


def setup_inputs(seed: int = 0) -> dict:
    key = jax.random.key(seed)
    ks = iter(jax.random.split(key, 32))
    f32 = jnp.float32
    ne = (DEPTH + 1) // 2
    no = DEPTH // 2

    def dense(shape, fan_in):
        return jax.random.normal(next(ks), shape, f32) * (fan_in ** -0.5)

    def gain(shape):
        return 1.0 + 0.05 * jax.random.normal(next(ks), shape, f32)

    x = jax.random.normal(next(ks), (BATCH, SEQ, D_MODEL), f32)
    a_init = jax.random.uniform(next(ks), (ne, DN_HEADS), f32, 1.0, 16.0)
    dt = jnp.exp(jax.random.uniform(next(ks), (ne, DN_HEADS), f32,
                                    float(np.log(1e-3)), float(np.log(1e-1))))
    return {
        "x": x,
        "ev_norm1": gain((ne, D_MODEL)),
        "ev_w_in": dense((ne, D_MODEL, EVEN_IN), D_MODEL),
        "ev_conv_qkv": dense((ne, DN_CONV, 3 * DN_WIDTH), DN_CONV),
        "ev_a_log": jnp.log(a_init),
        "ev_dt_bias": dt + jnp.log(-jnp.expm1(-dt)),
        "ev_o_norm": gain((ne, DN_HEAD_DIM)),
        "ev_conv_sc": dense((ne, SC_CONV, SC_WIDTH), SC_CONV),
        "ev_w_out": dense((ne, EVEN_MIX, D_MODEL), EVEN_MIX),
        "ev_norm2": gain((ne, D_MODEL)),
        "ev_ffn_w1": dense((ne, D_MODEL, FFN_DIM), D_MODEL),
        "ev_ffn_w3": dense((ne, D_MODEL, FFN_DIM), D_MODEL),
        "ev_ffn_w2": dense((ne, FFN_DIM, D_MODEL), FFN_DIM),
        "od_norm1": gain((no, D_MODEL)),
        "od_w_in": dense((no, D_MODEL, ODD_IN), D_MODEL),
        "od_q_norm": gain((no, SWA_HEAD_DIM)),
        "od_k_norm": gain((no, SWA_HEAD_DIM)),
        "od_sinks": 0.5 * jax.random.normal(next(ks), (no, SWA_Q_HEADS), f32),
        "od_w_out": dense((no, ODD_MIX, D_MODEL), ODD_MIX),
        "od_norm2": gain((no, D_MODEL)),
        "od_router": dense((no, D_MODEL, N_EXPERTS), D_MODEL),
        "od_moe_w1": dense((no, N_EXPERTS, D_MODEL, EXPERT_DIM), D_MODEL),
        "od_moe_w3": dense((no, N_EXPERTS, D_MODEL, EXPERT_DIM), D_MODEL),
        "od_moe_w2": dense((no, N_EXPERTS, EXPERT_DIM, D_MODEL), EXPERT_DIM),
    }


def reference(x, ev_norm1, ev_w_in, ev_conv_qkv, ev_a_log, ev_dt_bias, ev_o_norm,
              ev_conv_sc, ev_w_out, ev_norm2, ev_ffn_w1, ev_ffn_w3, ev_ffn_w2,
              od_norm1, od_w_in, od_q_norm, od_k_norm, od_sinks, od_w_out, od_norm2,
              od_router, od_moe_w1, od_moe_w3, od_moe_w2):
    for layer in range(DEPTH):
        i = layer // 2
        if layer % 2 == 0:
            x = x + even_mixer(rms_norm(x, ev_norm1[i]), ev_w_in[i], ev_conv_qkv[i],
                               ev_a_log[i], ev_dt_bias[i], ev_o_norm[i], ev_conv_sc[i],
                               ev_w_out[i])
            x = x + swiglu(rms_norm(x, ev_norm2[i]), ev_ffn_w1[i], ev_ffn_w3[i], ev_ffn_w2[i])
        else:
            x = x + odd_mixer(rms_norm(x, od_norm1[i]), od_w_in[i], od_q_norm[i],
                              od_k_norm[i], od_sinks[i], od_w_out[i])
            x = x + moe_swiglu(rms_norm(x, od_norm2[i]), od_router[i], od_moe_w1[i],
                               od_moe_w3[i], od_moe_w2[i])
    return x
```

```python
import functools

import jax
import jax.numpy as jnp
from jax import lax
from jax.experimental import pallas as pl
from jax.experimental.pallas import tpu as pltpu

F32 = jnp.float32
BF16 = jnp.bfloat16

D_MODEL = 1024
RMS_EPS = 1e-6
L2_EPS = 1e-6
NEG_INF = -1e30

DN_HEADS = 4
DN_HEAD_DIM = 128
DN_WIDTH = DN_HEADS * DN_HEAD_DIM
DN_CONV = 4
DN_CHUNK = 64
SC_WIDTH = 512
SC_CONV = 3

SWA_Q_HEADS = 16
SWA_KV_HEADS = 2
SWA_HEAD_DIM = 64
SWA_WINDOW = 128
SWA_GROUP = SWA_Q_HEADS // SWA_KV_HEADS

N_EXPERTS = 8
FFN_DIM = 3584

LANES = 128
SUBLANES = 8
PAIR = 2 * DN_CHUNK

VMEM_LIMIT = 56 * 1024 * 1024


def _dot(a, b):
    return jnp.dot(a, b, preferred_element_type=F32)


def _dot_nt(a, b):
    return lax.dot_general(a, b, (((1,), (1,)), ((), ())), preferred_element_type=F32)


def _split2(x):
    hi = x.astype(BF16)
    lo = (x - hi.astype(F32)).astype(BF16)
    return hi, lo


def _split3(x):
    hi = x.astype(BF16)
    r = x - hi.astype(F32)
    mid = r.astype(BF16)
    lo = (r - mid.astype(F32)).astype(BF16)
    return hi, mid, lo


def _dot_x3(a, b):
    ah, al = _split2(a)
    bh, bl = _split2(b)
    return _dot(ah, bh) + _dot(al, bh) + _dot(ah, bl)


def _dot_mask_l(mask_bf16, x):
    h, m, l = _split3(x)
    return _dot(mask_bf16, h) + _dot(mask_bf16, m) + _dot(mask_bf16, l)


def _dot_mask_r(x, mask_bf16):
    h, m, l = _split3(x)
    return _dot(h, mask_bf16) + _dot(m, mask_bf16) + _dot(l, mask_bf16)


def _silu(x):
    return x * (1.0 / (1.0 + jnp.exp(-x)))


def _rms_rows(x, gain):
    ms = jnp.mean(x * x, axis=-1, keepdims=True)
    return x * lax.rsqrt(ms + RMS_EPS) * gain


def _inproj_kernel(x_ref, g_ref, w_ref, *rest, n_chunk, has_ab):
    if has_ab:
        wab_hi_ref, wab_lo_ref, o_ref, oab_ref = rest
    else:
        (o_ref,) = rest
    h = _rms_rows(x_ref[...], g_ref[...])
    hb = h.astype(BF16)
    n = o_ref.shape[1]
    for c in range(0, n, n_chunk):
        o_ref[:, c:c + n_chunk] = _dot(hb, w_ref[:, c:c + n_chunk]).astype(o_ref.dtype)
    if has_ab:
        hlo = (h - hb.astype(F32)).astype(BF16)
        whi = wab_hi_ref[...]
        oab_ref[...] = _dot(hb, whi) + _dot(hlo, whi) + _dot(hb, wab_lo_ref[...])


def _inproj(x2d, gain, w_bf16, wab=None, *, tm=512, n_chunk=256):
    t, d = x2d.shape
    n = w_bf16.shape[1]
    has_ab = wab is not None
    in_specs = [
        pl.BlockSpec((tm, d), lambda i: (i, 0)),
        pl.BlockSpec((1, d), lambda i: (0, 0)),
        pl.BlockSpec((d, n), lambda i: (0, 0)),
    ]
    args = [x2d, gain.reshape(1, d), w_bf16]
    out_shape = [jax.ShapeDtypeStruct((t, n), BF16)]
    out_specs = [pl.BlockSpec((tm, n), lambda i: (i, 0))]
    if has_ab:
        wab_hi = wab.astype(BF16)
        wab_lo = (wab - wab_hi.astype(F32)).astype(BF16)
        in_specs += [pl.BlockSpec((d, LANES), lambda i: (0, 0))] * 2
        args += [wab_hi, wab_lo]
        out_shape.append(jax.ShapeDtypeStruct((t, LANES), F32))
        out_specs.append(pl.BlockSpec((tm, LANES), lambda i: (i, 0)))
    res = pl.pallas_call(
        functools.partial(_inproj_kernel, n_chunk=n_chunk, has_ab=has_ab),
        grid=(t // tm,),
        in_specs=in_specs,
        out_specs=out_specs,
        out_shape=out_shape,
        compiler_params=pltpu.CompilerParams(
            dimension_semantics=("arbitrary",), vmem_limit_bytes=VMEM_LIMIT),
        name="inproj_ab" if has_ab else "inproj",
    )(*args)
    return res if has_ab else res[0]


def _outproj_kernel(a_ref, w_ref, r_ref, o_ref):
    o_ref[...] = r_ref[...] + _dot(a_ref[...], w_ref[...])


def _outproj(a_bf16, w_bf16, res, *, tm=512):
    t, k = a_bf16.shape
    n = w_bf16.shape[1]
    return pl.pallas_call(
        _outproj_kernel,
        grid=(t // tm,),
        in_specs=[
            pl.BlockSpec((tm, k), lambda i: (i, 0)),
            pl.BlockSpec((k, n), lambda i: (0, 0)),
            pl.BlockSpec((tm, n), lambda i: (i, 0)),
        ],
        out_specs=pl.BlockSpec((tm, n), lambda i: (i, 0)),
        out_shape=jax.ShapeDtypeStruct((t, n), F32),
        compiler_params=pltpu.CompilerParams(
            dimension_semantics=("arbitrary",), vmem_limit_bytes=VMEM_LIMIT),
        name="outproj",
    )(a_bf16, w_bf16, res)


def _deltanet_kernel(p_ref, halo_ref, ab_ref, cq_ref, cs_ref, alog_ref, dtb_ref, onorm_ref,
                     sel_ref, o_ref,
                     xe_scr, q_scr, k_scr, v_scr, g_scr, b_scr, oh_scr, s_scr, *, ts):
    s_idx = pl.program_id(1)
    w4 = DN_WIDTH
    ncv = 3 * w4 + SC_WIDTH

    @pl.when(s_idx == 0)
    def _():
        s_scr[...] = jnp.zeros_like(s_scr)

    def conv_inputs(ref, rows):
        qkv = ref[rows, 0:3 * w4].astype(F32)
        cg = ref[rows, 5 * w4:6 * w4].astype(F32)
        xin = ref[rows, 6 * w4:7 * w4].astype(F32)
        return qkv, cg * xin

    hq, hs = conv_inputs(halo_ref, slice(None))
    live = (s_idx > 0).astype(F32)
    xe_scr[0:SUBLANES, 0:3 * w4] = hq * live
    xe_scr[0:SUBLANES, 3 * w4:ncv] = hs * live
    tq, tsc = conv_inputs(p_ref, slice(None))
    xe_scr[SUBLANES:SUBLANES + ts, 0:3 * w4] = tq
    xe_scr[SUBLANES:SUBLANES + ts, 3 * w4:ncv] = tsc

    def conv(cols, w_ref, width):
        acc = None
        for j in range(width):
            off = SUBLANES - (width - 1) + j
            term = xe_scr[off:off + ts, cols] * w_ref[j:j + 1, :]
            acc = term if acc is None else acc + term
        return acc

    lane = lax.broadcasted_iota(jnp.int32, (LANES, LANES), 1)
    row = lax.broadcasted_iota(jnp.int32, (LANES, LANES), 0)

    for part, scr in ((0, q_scr), (1, k_scr), (2, v_scr)):
        cols = slice(part * w4, (part + 1) * w4)
        y = _silu(conv(cols, cq_ref.at[:, cols], DN_CONV))
        if part < 2:
            for h in range(DN_HEADS):
                hc = slice(h * DN_HEAD_DIM, (h + 1) * DN_HEAD_DIM)
                yh = y[:, hc]
                inv = lax.rsqrt(jnp.sum(yh * yh, axis=-1, keepdims=True) + L2_EPS)
                if part == 0:
                    inv = inv * (DN_HEAD_DIM ** -0.5)
                scr[:, hc] = yh * inv
        else:
            scr[...] = y

    abb = _dot_mask_r(ab_ref[...], sel_ref[...])
    for h in range(DN_HEADS):
        hc = slice(h * LANES, (h + 1) * LANES)
        a_b = abb[:, hc]
        b_b = abb[:, (DN_HEADS + h) * LANES:(DN_HEADS + h + 1) * LANES]
        z = a_b + dtb_ref[h]
        softplus = jnp.maximum(z, 0.0) + jnp.log1p(jnp.exp(-jnp.abs(z)))
        g_scr[:, hc] = -jnp.exp(jnp.full((1, LANES), alog_ref[h], F32)) * softplus
        b_scr[:, hc] = 1.0 / (1.0 + jnp.exp(-b_b))

    same = (row // DN_CHUNK) == (lane // DN_CHUNK)
    incl = same & (row >= lane)
    strict = same & (row > lane)
    cs_mask = incl.astype(BF16)
    eye = (row == lane).astype(F32)
    lvl_masks = []
    blk = 2
    while blk < DN_CHUNK:
        lvl_masks.append(((row // (2 * blk)) == (lane // (2 * blk)))
                         & ((row // blk) != (lane // blk)) & (row > lane))
        blk *= 2
    lvl1 = ((row // 2) == (lane // 2)) & (row > lane)
    first_chunk_col = lane < DN_CHUNK

    def pair_body(p, carry):
        r0 = pl.multiple_of(p * PAIR, PAIR)
        rows = pl.ds(r0, PAIR)
        for h in range(DN_HEADS):
            hc = slice(h * DN_HEAD_DIM, (h + 1) * DN_HEAD_DIM)
            qn = q_scr[rows, hc]
            kn = k_scr[rows, hc]
            vv = v_scr[rows, hc]
            gb = g_scr[rows, hc]
            beta = b_scr[rows, hc]
            gc = _dot_mask_l(cs_mask, gb)
            diff = gc - gc.T
            dec = jnp.where(incl, jnp.exp(jnp.where(incl, diff, 0.0)), 0.0)
            egc = jnp.exp(gc)
            kb = kn * beta
            kn_b = kn.astype(BF16)
            kk = _dot_nt(kb.astype(BF16), kn_b)
            lmat = jnp.where(strict, kk * dec, 0.0)
            tinv = eye - jnp.where(lvl1, lmat, 0.0)
            for m in lvl_masks:
                x = _dot_x3(tinv, jnp.where(m, lmat, 0.0))
                tinv = tinv - _dot_x3(x, tinv)
            tinv_b = tinv.astype(BF16)
            u = _dot(tinv_b, (vv * beta).astype(BF16))
            w = _dot(tinv_b, (kb * egc).astype(BF16))
            qd = (qn * egc).astype(BF16)
            attn = jnp.where(incl, _dot_nt(qn.astype(BF16), kn_b) * dec, 0.0).astype(BF16)
            gl0 = gc[DN_CHUNK - 1:DN_CHUNK, :]
            gl1 = gc[PAIR - 1:PAIR, :]
            glast = jnp.where(row < DN_CHUNK, gl0, gl1)
            kdt = (kn * jnp.exp(glast - gc)).T
            kdt0 = jnp.where(first_chunk_col, kdt, 0.0).astype(BF16)
            kdt1 = jnp.where(first_chunk_col, 0.0, kdt).astype(BF16)
            w_b = w.astype(BF16)

            st = s_scr[h]
            st_b = st.astype(BF16)
            vn0 = u - _dot(w_b, st_b)
            o0 = _dot(qd, st_b)
            vn0 = jnp.where(row < DN_CHUNK, vn0, 0.0)
            st = st * jnp.exp(gl0) + _dot(kdt0, vn0.astype(BF16))
            st_b = st.astype(BF16)
            vn1 = u - _dot(w_b, st_b)
            o1 = _dot(qd, st_b)
            vn = jnp.where(row < DN_CHUNK, vn0, vn1)
            vn_b = vn.astype(BF16)
            st = st * jnp.exp(gl1) + _dot(kdt1, vn_b)
            s_scr[h] = st
            oh_scr[rows, hc] = jnp.where(row < DN_CHUNK, o0, o1) + _dot(attn, vn_b)
        return carry

    lax.fori_loop(0, ts // PAIR, pair_body, 0)

    for h in range(DN_HEADS):
        hc = slice(h * DN_HEAD_DIM, (h + 1) * DN_HEAD_DIM)
        o = _rms_rows(oh_scr[:, hc], onorm_ref[...])
        zg = p_ref[:, 3 * w4 + h * DN_HEAD_DIM:3 * w4 + (h + 1) * DN_HEAD_DIM].astype(F32)
        o_ref[:, hc] = (o * _silu(zg)).astype(o_ref.dtype)
    bg = p_ref[:, 4 * w4:5 * w4].astype(F32)
    ysc = bg * conv(slice(3 * w4, ncv), cs_ref, SC_CONV)
    o_ref[:, w4:w4 + SC_WIDTH] = ysc.astype(o_ref.dtype)


def _deltanet_mix(proj, ab, conv_qkv, a_log, dt_bias, o_norm, conv_sc, *, ts=512):
    b, s, n = proj.shape
    w4 = DN_WIDTH
    rpb = ts // SUBLANES
    sel = (jnp.arange(LANES)[:, None] == (jnp.arange(2 * DN_HEADS * LANES)[None, :] // LANES)
           ).astype(BF16)
    kern = functools.partial(_deltanet_kernel, ts=ts)
    smem = pl.BlockSpec(memory_space=pltpu.SMEM)
    return pl.pallas_call(
        kern,
        grid=(b, s // ts),
        in_specs=[
            pl.BlockSpec((None, ts, n), lambda i, j: (i, j, 0)),
            pl.BlockSpec((None, SUBLANES, n), lambda i, j: (i, jnp.maximum(j * rpb - 1, 0), 0)),
            pl.BlockSpec((None, ts, LANES), lambda i, j: (i, j, 0)),
            pl.BlockSpec((DN_CONV, 3 * w4), lambda i, j: (0, 0)),
            pl.BlockSpec((SC_CONV, SC_WIDTH), lambda i, j: (0, 0)),
            smem, smem,
            pl.BlockSpec((1, DN_HEAD_DIM), lambda i, j: (0, 0)),
            pl.BlockSpec((LANES, 2 * DN_HEADS * LANES), lambda i, j: (0, 0)),
        ],
        out_specs=pl.BlockSpec((None, ts, w4 + SC_WIDTH), lambda i, j: (i, j, 0)),
        out_shape=jax.ShapeDtypeStruct((b, s, w4 + SC_WIDTH), BF16),
        scratch_shapes=[
            pltpu.VMEM((ts + SUBLANES, 3 * w4 + SC_WIDTH), F32),
            pltpu.VMEM((ts, w4), F32), pltpu.VMEM((ts, w4), F32), pltpu.VMEM((ts, w4), F32),
            pltpu.VMEM((ts, w4), F32), pltpu.VMEM((ts, w4), F32), pltpu.VMEM((ts, w4), F32),
            pltpu.VMEM((DN_HEADS, DN_HEAD_DIM, DN_HEAD_DIM), F32),
        ],
        compiler_params=pltpu.CompilerParams(
            dimension_semantics=("arbitrary", "arbitrary"), vmem_limit_bytes=VMEM_LIMIT),
        name="deltanet",
    )(proj, proj, ab, conv_qkv, conv_sc, a_log, dt_bias, o_norm.reshape(1, -1), sel)


def _swa_kernel(q_ref, kv_ref, halo_ref, qw_ref, kw_ref, sink_ref, bd_ref, o_ref,
                qn_scr, kw_scr, vw_scr, *, tq):
    s_idx = pl.program_id(1)
    blk = SWA_WINDOW
    hd = SWA_HEAD_DIM
    bd = bd_ref[...]

    def head_rms(x, gain):
        hi, lo = _split2(x * x)
        ms = (_dot(hi, bd) + _dot(lo, bd)) * (1.0 / hd)
        return x * lax.rsqrt(ms + RMS_EPS) * gain

    for c in range(SWA_Q_HEADS * hd // LANES):
        cs = slice(c * LANES, (c + 1) * LANES)
        qc = head_rms(q_ref[:, cs].astype(F32), qw_ref[...]) * (hd ** -0.5)
        qn_scr[:, cs] = qc.astype(BF16)

    live = (s_idx > 0).astype(F32)
    kw_scr[0:blk, :] = head_rms(halo_ref[:, 0:LANES].astype(F32), kw_ref[...]) * live
    kw_scr[blk:blk + tq, :] = head_rms(kv_ref[:, 0:LANES].astype(F32), kw_ref[...])
    vw_scr[0:blk, :] = halo_ref[:, LANES:2 * LANES].astype(F32) * live
    vw_scr[blk:blk + tq, :] = kv_ref[:, LANES:2 * LANES].astype(F32)

    qi = lax.broadcasted_iota(jnp.int32, (blk, 2 * blk), 0)
    kj = lax.broadcasted_iota(jnp.int32, (blk, 2 * blk), 1)
    band = (kj > qi) & (kj <= qi + blk)
    lane = lax.broadcasted_iota(jnp.int32, (2 * blk, LANES), 1)
    lo_half = lane < hd

    def block_body(j, carry):
        r0 = pl.multiple_of(j * blk, blk)
        valid = band & ((kj >= blk) | (s_idx > 0) | (j > 0))
        kwin = kw_scr[pl.ds(r0, 2 * blk), :]
        vwin = vw_scr[pl.ds(r0, 2 * blk), :]
        kswap = pltpu.roll(kwin, hd, axis=1)
        vswap = pltpu.roll(vwin, hd, axis=1)
        for g in range(SWA_KV_HEADS):
            keep = lo_half if g == 0 else jnp.logical_not(lo_half)
            k_nat = jnp.where(keep, kwin, 0.0).astype(BF16)
            k_swp = jnp.where(keep, 0.0, kswap).astype(BF16)
            v_nat = jnp.where(keep, vwin, 0.0).astype(BF16)
            v_swp = jnp.where(keep, 0.0, vswap).astype(BF16)
            k_lo, k_hi = (k_nat, k_swp) if g == 0 else (k_swp, k_nat)
            v_lo, v_hi = (v_nat, v_swp) if g == 0 else (v_swp, v_nat)
            for c in range(SWA_GROUP // 2):
                cc = g * (SWA_GROUP // 2) + c
                cs = slice(cc * LANES, (cc + 1) * LANES)
                q2 = qn_scr[pl.ds(r0, blk), cs]
                acc = None
                for half, (kx, vx) in enumerate(((k_lo, v_lo), (k_hi, v_hi))):
                    head = 2 * cc + half
                    sink = sink_ref[head]
                    sc = jnp.where(valid, _dot_nt(q2, kx), NEG_INF)
                    m = jnp.maximum(jnp.max(sc, axis=-1, keepdims=True), sink)
                    pexp = jnp.exp(sc - m)
                    den = jnp.sum(pexp, axis=-1, keepdims=True) + jnp.exp(sink - m)
                    pr = (pexp * (1.0 / den)).astype(BF16)
                    term = _dot(pr, vx)
                    acc = term if acc is None else acc + term
                o_ref[pl.ds(r0, blk), cs] = acc.astype(o_ref.dtype)
        return carry

    lax.fori_loop(0, tq // blk, block_body, 0)


def _swa(qkv, q_norm, k_norm, sinks, *, tq=512):
    b, s, _ = qkv.shape
    nq = SWA_Q_HEADS * SWA_HEAD_DIM
    nkv = 2 * SWA_KV_HEADS * SWA_HEAD_DIM
    blk = SWA_WINDOW
    qw = jnp.tile(q_norm, LANES // SWA_HEAD_DIM).reshape(1, LANES)
    kw = jnp.tile(k_norm, LANES // SWA_HEAD_DIM).reshape(1, LANES)
    ids = jnp.arange(LANES) // SWA_HEAD_DIM
    bd = (ids[:, None] == ids[None, :]).astype(BF16)
    return pl.pallas_call(
        functools.partial(_swa_kernel, tq=tq),
        grid=(b, s // tq),
        in_specs=[
            pl.BlockSpec((None, tq, nq), lambda i, j: (i, j, 0)),
            pl.BlockSpec((None, tq, nkv), lambda i, j: (i, j, nq // nkv)),
            pl.BlockSpec((None, blk, nkv),
                         lambda i, j: (i, jnp.maximum(j * (tq // blk) - 1, 0), nq // nkv)),
            pl.BlockSpec((1, LANES), lambda i, j: (0, 0)),
            pl.BlockSpec((1, LANES), lambda i, j: (0, 0)),
            pl.BlockSpec(memory_space=pltpu.SMEM),
            pl.BlockSpec((LANES, LANES), lambda i, j: (0, 0)),
        ],
        out_specs=pl.BlockSpec((None, tq, nq), lambda i, j: (i, j, 0)),
        out_shape=jax.ShapeDtypeStruct((b, s, nq), BF16),
        scratch_shapes=[
            pltpu.VMEM((tq, nq), BF16),
            pltpu.VMEM((blk + tq, LANES), F32),
            pltpu.VMEM((blk + tq, LANES), F32),
        ],
        compiler_params=pltpu.CompilerParams(
            dimension_semantics=("arbitrary", "arbitrary"), vmem_limit_bytes=VMEM_LIMIT),
        name="swa",
    )(qkv, qkv, qkv, qw, kw, sinks, bd)


def _ffn_kernel(te_ref, nused_ref, x_ref, g_ref, w1_ref, w3_ref, w2_ref, o_ref, h_scr, acc_scr,
                *, add_residual, nf):
    i = pl.program_id(0)
    f = pl.program_id(1)
    used = i < nused_ref[0]

    @pl.when(used & (f == 0))
    def _():
        h_scr[...] = _rms_rows(x_ref[...], g_ref[...]).astype(BF16)
        acc_scr[...] = jnp.zeros_like(acc_scr)

    @pl.when(used)
    def _():
        hb = h_scr[...]
        a = _dot(hb, w1_ref[...])
        b = _dot(hb, w3_ref[...])
        acc_scr[...] += _dot((_silu(a) * b).astype(BF16), w2_ref[...])

    @pl.when(f == nf - 1)
    def _():
        @pl.when(used)
        def _():
            y = acc_scr[...]
            o_ref[...] = (x_ref[...] + y) if add_residual else y

        @pl.when(jnp.logical_not(used))
        def _():
            o_ref[...] = jnp.zeros_like(o_ref)


def _ffn(tile_expert, n_used, x2d, gain, w1, w3, w2, *, add_residual, tm=512, tf=512):
    r, d = x2d.shape
    fdim = w1.shape[2]
    nf = fdim // tf

    def fsel(i, f, nused):
        return jnp.where(i < nused[0], f, 0)

    grid_spec = pltpu.PrefetchScalarGridSpec(
        num_scalar_prefetch=2,
        grid=(r // tm, nf),
        in_specs=[
            pl.BlockSpec((tm, d), lambda i, f, te, nu: (i, 0)),
            pl.BlockSpec((1, d), lambda i, f, te, nu: (0, 0)),
            pl.BlockSpec((None, d, tf), lambda i, f, te, nu: (te[i], 0, fsel(i, f, nu))),
            pl.BlockSpec((None, d, tf), lambda i, f, te, nu: (te[i], 0, fsel(i, f, nu))),
            pl.BlockSpec((None, tf, d), lambda i, f, te, nu: (te[i], fsel(i, f, nu), 0)),
        ],
        out_specs=pl.BlockSpec((tm, d), lambda i, f, te, nu: (i, 0)),
        scratch_shapes=[pltpu.VMEM((tm, d), BF16), pltpu.VMEM((tm, d), F32)],
    )
    return pl.pallas_call(
        functools.partial(_ffn_kernel, add_residual=add_residual, nf=nf),
        grid_spec=grid_spec,
        out_shape=jax.ShapeDtypeStruct((r, d), F32),
        compiler_params=pltpu.CompilerParams(
            dimension_semantics=("arbitrary", "arbitrary"), vmem_limit_bytes=VMEM_LIMIT),
        name="ffn_res" if add_residual else "ffn_experts",
    )(tile_expert, n_used, x2d, gain.reshape(1, d), w1, w3, w2)


def _router_kernel(x_ref, g_ref, whi_ref, wlo_ref, tri_ref, info_ref, cnt_ref, carry_scr):
    i = pl.program_id(0)

    @pl.when(i == 0)
    def _():
        carry_scr[...] = jnp.zeros_like(carry_scr)

    h = _rms_rows(x_ref[...], g_ref[...])
    hhi, hlo = _split2(h)
    whi = whi_ref[...]
    logits = _dot(hhi, whi) + _dot(hlo, whi) + _dot(hhi, wlo_ref[...])
    lane = lax.broadcasted_iota(jnp.int32, logits.shape, 1).astype(F32)
    lg = jnp.where(lane < N_EXPERTS, logits, -jnp.inf)
    v1 = jnp.max(lg, axis=-1, keepdims=True)
    i1 = jnp.min(jnp.where(lg == v1, lane, float(LANES)), axis=-1, keepdims=True)
    lg2 = jnp.where(lane == i1, -jnp.inf, lg)
    v2 = jnp.max(lg2, axis=-1, keepdims=True)
    i2 = jnp.min(jnp.where(lg2 == v2, lane, float(LANES)), axis=-1, keepdims=True)
    e = jnp.exp(v2 - v1)
    gate1 = 1.0 / (1.0 + e)
    gate2 = e / (1.0 + e)
    m1 = lane == i1
    m2 = lane == i2
    member = (m1 | m2).astype(F32)
    carry = carry_scr[0:1, :]
    rank = _dot(tri_ref[...], member.astype(BF16)) + carry
    r1 = jnp.sum(jnp.where(m1, rank, 0.0), axis=-1, keepdims=True)
    r2 = jnp.sum(jnp.where(m2, rank, 0.0), axis=-1, keepdims=True)
    total = carry + jnp.sum(member, axis=0, keepdims=True)
    carry_scr[...] = jnp.broadcast_to(total, carry_scr.shape)
    cnt_ref[...] = jnp.broadcast_to(total, cnt_ref.shape)
    info = jnp.where(lane == 0, i1, 0.0)
    info = jnp.where(lane == 1, i2, info)
    info = jnp.where(lane == 2, gate1, info)
    info = jnp.where(lane == 3, gate2, info)
    info = jnp.where(lane == 4, r1, info)
    info = jnp.where(lane == 5, r2, info)
    info_ref[...] = info


def _router(x2d, gain, w_router, *, tm=512):
    t, d = x2d.shape
    wpad = jnp.zeros((d, LANES), F32).at[:, :N_EXPERTS].set(w_router)
    whi = wpad.astype(BF16)
    wlo = (wpad - whi.astype(F32)).astype(BF16)
    tri = (jnp.arange(tm)[:, None] > jnp.arange(tm)[None, :]).astype(BF16)
    return pl.pallas_call(
        _router_kernel,
        grid=(t // tm,),
        in_specs=[
            pl.BlockSpec((tm, d), lambda i: (i, 0)),
            pl.BlockSpec((1, d), lambda i: (0, 0)),
            pl.BlockSpec((d, LANES), lambda i: (0, 0)),
            pl.BlockSpec((d, LANES), lambda i: (0, 0)),
            pl.BlockSpec((tm, tm), lambda i: (0, 0)),
        ],
        out_specs=[
            pl.BlockSpec((tm, LANES), lambda i: (i, 0)),
            pl.BlockSpec((SUBLANES, LANES), lambda i: (0, 0)),
        ],
        out_shape=[
            jax.ShapeDtypeStruct((t, LANES), F32),
            jax.ShapeDtypeStruct((SUBLANES, LANES), F32),
        ],
        scratch_shapes=[pltpu.VMEM((SUBLANES, LANES), F32)],
        compiler_params=pltpu.CompilerParams(
            dimension_semantics=("arbitrary",), vmem_limit_bytes=VMEM_LIMIT),
        name="router",
    )(x2d, gain.reshape(1, d), whi, wlo, tri)


def _row_copy(src, dst, sem):
    return pltpu.make_async_copy(src, dst, sem)


def _dispatch_kernel(pos_ref, x_ref, init_ref, xs_ref, sem, *, tm):
    del init_ref

    def copies(r):
        for k in range(2):
            yield _row_copy(x_ref.at[pl.ds(r, 1)], xs_ref.at[pl.ds(pos_ref[0, 0, 2 * r + k], 1)], sem)

    def start(r, c):
        for cp in copies(r):
            cp.start()
        return c

    def wait(r, c):
        for cp in copies(r):
            cp.wait()
        return c

    lax.fori_loop(0, tm, start, 0)
    lax.fori_loop(0, tm, wait, 0)


def _dispatch(pos, x2d, n_rows, *, tm=256):
    t, d = x2d.shape
    pos3 = pos.reshape(t // tm, 1, 2 * tm)
    return pl.pallas_call(
        functools.partial(_dispatch_kernel, tm=tm),
        grid=(t // tm,),
        in_specs=[
            pl.BlockSpec((1, 1, 2 * tm), lambda i: (i, 0, 0), memory_space=pltpu.SMEM),
            pl.BlockSpec((tm, d), lambda i: (i, 0)),
            pl.BlockSpec(memory_space=pl.ANY),
        ],
        out_specs=pl.BlockSpec(memory_space=pl.ANY),
        out_shape=jax.ShapeDtypeStruct((n_rows, d), F32),
        scratch_shapes=[pltpu.SemaphoreType.DMA],
        input_output_aliases={2: 0},
        compiler_params=pltpu.CompilerParams(
            dimension_semantics=("arbitrary",), vmem_limit_bytes=VMEM_LIMIT),
        name="dispatch",
    )(pos3, x2d, jnp.zeros((n_rows, d), F32))


def _combine_kernel(pos_ref, x_ref, gate_ref, y_ref, o_ref, buf, sem, *, tm):
    def copies(r):
        for k in range(2):
            yield _row_copy(y_ref.at[pl.ds(pos_ref[0, 0, 2 * r + k], 1)],
                            buf.at[k, pl.ds(r, 1)], sem)

    def start(r, c):
        for cp in copies(r):
            cp.start()
        return c

    def wait(r, c):
        for cp in copies(r):
            cp.wait()
        return c

    lax.fori_loop(0, tm, start, 0)
    lax.fori_loop(0, tm, wait, 0)
    g = gate_ref[...]
    o_ref[...] = x_ref[...] + g[:, 2:3] * buf[0] + g[:, 3:4] * buf[1]


def _combine(pos, x2d, info, y, *, tm=256):
    t, d = x2d.shape
    pos3 = pos.reshape(t // tm, 1, 2 * tm)
    return pl.pallas_call(
        functools.partial(_combine_kernel, tm=tm),
        grid=(t // tm,),
        in_specs=[
            pl.BlockSpec((1, 1, 2 * tm), lambda i: (i, 0, 0), memory_space=pltpu.SMEM),
            pl.BlockSpec((tm, d), lambda i: (i, 0)),
            pl.BlockSpec((tm, LANES), lambda i: (i, 0)),
            pl.BlockSpec(memory_space=pl.ANY),
        ],
        out_specs=pl.BlockSpec((tm, d), lambda i: (i, 0)),
        out_shape=jax.ShapeDtypeStruct((t, d), F32),
        scratch_shapes=[pltpu.VMEM((2, tm, d), F32), pltpu.SemaphoreType.DMA],
        compiler_params=pltpu.CompilerParams(
            dimension_semantics=("arbitrary",), vmem_limit_bytes=VMEM_LIMIT),
        name="combine",
    )(pos3, x2d, info, y)


def _even_layer(x2d, bsz, seqlen, norm1, w_in, conv_qkv, a_log, dt_bias, o_norm, conv_sc, w_out,
                norm2, w1, w3, w2):
    w4 = DN_WIDTH
    c_ab = 4 * w4
    c_sc = c_ab + 2 * DN_HEADS
    w_main = jnp.concatenate([w_in[:, :c_ab], w_in[:, c_sc:]], axis=1).astype(BF16)
    w_ab = jnp.zeros((D_MODEL, LANES), F32).at[:, :2 * DN_HEADS].set(w_in[:, c_ab:c_sc])
    proj, ab = _inproj(x2d, norm1, w_main, w_ab)
    mix = _deltanet_mix(proj.reshape(bsz, seqlen, -1), ab.reshape(bsz, seqlen, LANES),
                        conv_qkv, a_log, dt_bias, o_norm, conv_sc)
    x2d = _outproj(mix.reshape(bsz * seqlen, -1), w_out.astype(BF16), x2d)
    n_tiles = x2d.shape[0] // 512
    te = jnp.zeros((n_tiles,), jnp.int32)
    nused = jnp.full((1,), n_tiles, jnp.int32)
    return _ffn(te, nused, x2d, norm2, w1[None].astype(BF16), w3[None].astype(BF16),
                w2[None].astype(BF16), add_residual=True)


def _odd_layer(x2d, bsz, seqlen, norm1, w_in, q_norm, k_norm, sinks, w_out, norm2, w_router,
               w1, w3, w2, *, tm=512):
    t = x2d.shape[0]
    qkv = _inproj(x2d, norm1, w_in.astype(BF16))
    att = _swa(qkv.reshape(bsz, seqlen, -1), q_norm, k_norm, sinks)
    x2d = _outproj(att.reshape(t, -1), w_out.astype(BF16), x2d)

    info, cnt = _router(x2d, norm2, w_router)
    counts = cnt[0, :N_EXPERTS].astype(jnp.int32)
    padded = ((counts + tm - 1) // tm) * tm
    ends = jnp.cumsum(padded)
    offsets = ends - padded
    n_rows = 2 * t + N_EXPERTS * tm
    n_tiles = n_rows // tm
    tile_expert = jnp.minimum(
        jnp.searchsorted(ends, jnp.arange(n_tiles, dtype=jnp.int32) * tm, side="right"),
        N_EXPERTS - 1).astype(jnp.int32)
    n_used = (ends[-1:] // tm).astype(jnp.int32)
    idx = info[:, 0:2].astype(jnp.int32)
    pos = (offsets[idx] + info[:, 4:6].astype(jnp.int32)).reshape(-1)

    xs = _dispatch(pos, x2d, n_rows)
    ys = _ffn(tile_expert, n_used, xs, norm2, w1.astype(BF16), w3.astype(BF16), w2.astype(BF16),
              add_residual=False, tm=tm)
    return _combine(pos, x2d, info, ys)


def kernel(x, ev_norm1, ev_w_in, ev_conv_qkv, ev_a_log, ev_dt_bias, ev_o_norm, ev_conv_sc,
           ev_w_out, ev_norm2, ev_ffn_w1, ev_ffn_w3, ev_ffn_w2, od_norm1, od_w_in, od_q_norm,
           od_k_norm, od_sinks, od_w_out, od_norm2, od_router, od_moe_w1, od_moe_w3, od_moe_w2):
    bsz, seqlen, d = x.shape
    x2d = x.reshape(bsz * seqlen, d)
    depth = ev_norm1.shape[0] + od_norm1.shape[0]
    for layer in range(depth):
        i = layer // 2
        if layer % 2 == 0:
            x2d = _even_layer(x2d, bsz, seqlen, ev_norm1[i], ev_w_in[i], ev_conv_qkv[i],
                              ev_a_log[i], ev_dt_bias[i], ev_o_norm[i], ev_conv_sc[i],
                              ev_w_out[i], ev_norm2[i], ev_ffn_w1[i], ev_ffn_w3[i], ev_ffn_w2[i])
        else:
            x2d = _odd_layer(x2d, bsz, seqlen, od_norm1[i], od_w_in[i], od_q_norm[i],
                             od_k_norm[i], od_sinks[i], od_w_out[i], od_norm2[i], od_router[i],
                             od_moe_w1[i], od_moe_w3[i], od_moe_w2[i])
    return x2d.reshape(bsz, seqlen, d)
```

```python
import functools

import jax
import jax.numpy as jnp
from jax import lax
from jax.experimental import pallas as pl
from jax.experimental.pallas import tpu as pltpu

F32 = jnp.float32
BF16 = jnp.bfloat16

D_MODEL = 1024
RMS_EPS = 1e-6
L2_EPS = 1e-6
NEG_INF = -1e30

DN_HEADS = 4
DN_HEAD_DIM = 128
DN_WIDTH = DN_HEADS * DN_HEAD_DIM
DN_CONV = 4
DN_CHUNK = 64
SC_WIDTH = 512
SC_CONV = 3

SWA_Q_HEADS = 16
SWA_KV_HEADS = 2
SWA_HEAD_DIM = 64
SWA_WINDOW = 128
SWA_GROUP = SWA_Q_HEADS // SWA_KV_HEADS

N_EXPERTS = 8

LANES = 128
SUBLANES = 8
PAIR = 2 * DN_CHUNK

TM = 512
TM_FFN = 1024
TF = 896
TD = 256
COPY_UNROLL = 8
VMEM_LIMIT = 56 * 1024 * 1024


def _dot(a, b):
    return jnp.dot(a, b, preferred_element_type=F32)


def _dot_nt(a, b):
    return lax.dot_general(a, b, (((1,), (1,)), ((), ())), preferred_element_type=F32)


def _split2(x):
    hi = x.astype(BF16)
    lo = (x - hi.astype(F32)).astype(BF16)
    return hi, lo


def _split3(x):
    hi = x.astype(BF16)
    r = x - hi.astype(F32)
    mid = r.astype(BF16)
    lo = (r - mid.astype(F32)).astype(BF16)
    return hi, mid, lo


def _dot_mask_l(mask_bf16, x):
    h, m, l = _split3(x)
    return _dot(mask_bf16, h) + _dot(mask_bf16, m) + _dot(mask_bf16, l)


def _silu(x):
    return x * (1.0 / (1.0 + jnp.exp(-x)))


def _rms_rows(x, gain):
    ms = jnp.mean(x * x, axis=-1, keepdims=True)
    return x * lax.rsqrt(ms + RMS_EPS) * gain


def _params(n_axes):
    return pltpu.CompilerParams(dimension_semantics=("arbitrary",) * n_axes,
                                vmem_limit_bytes=VMEM_LIMIT)


def _inproj_even_kernel(x_ref, g_ref, w_ref, wab_hi_ref, wab_lo_ref, cq_ref, cs_ref,
                        o_ref, oab_ref, xe_scr, *, tm, tiles_per_seq):
    w4 = DN_WIDTH
    ncv = 3 * w4 + SC_WIDTH
    h = _rms_rows(x_ref[...], g_ref[...])
    hb = h.astype(BF16)

    def proj(part):
        return _dot(hb, w_ref[:, part * w4:(part + 1) * w4])

    @pl.when(pl.program_id(0) == 0)
    def _():
        xe_scr[tm:tm + SUBLANES, :] = jnp.zeros((SUBLANES, ncv), F32)

    live = pl.program_id(0) % tiles_per_seq > 0
    xe_scr[0:SUBLANES, :] = jnp.where(live, xe_scr[tm:tm + SUBLANES, :], 0.0)
    for part in range(3):
        xe_scr[SUBLANES:SUBLANES + tm, part * w4:(part + 1) * w4] = proj(part)
    xe_scr[SUBLANES:SUBLANES + tm, 3 * w4:ncv] = proj(5) * proj(6)

    def conv(cols, wc_ref, width):
        acc = None
        for j in range(width):
            off = SUBLANES - (width - 1) + j
            term = xe_scr[off:off + tm, cols] * wc_ref[j:j + 1, :]
            acc = term if acc is None else acc + term
        return acc

    for part in range(3):
        cols = slice(part * w4, (part + 1) * w4)
        y = _silu(conv(cols, cq_ref.at[:, cols], DN_CONV))
        if part < 2:
            for hh in range(DN_HEADS):
                hc = slice(hh * DN_HEAD_DIM, (hh + 1) * DN_HEAD_DIM)
                yh = y[:, hc]
                inv = lax.rsqrt(jnp.sum(yh * yh, axis=-1, keepdims=True) + L2_EPS)
                if part == 0:
                    inv = inv * (DN_HEAD_DIM ** -0.5)
                o_ref[:, part * w4 + hh * DN_HEAD_DIM:part * w4 + (hh + 1) * DN_HEAD_DIM] = (
                    yh * inv).astype(o_ref.dtype)
        else:
            o_ref[:, cols] = y.astype(o_ref.dtype)
    o_ref[:, 3 * w4:4 * w4] = _silu(proj(3)).astype(o_ref.dtype)
    ysc = proj(4) * conv(slice(3 * w4, ncv), cs_ref, SC_CONV)
    o_ref[:, 4 * w4:5 * w4] = ysc.astype(o_ref.dtype)

    hlo = (h - hb.astype(F32)).astype(BF16)
    whi = wab_hi_ref[...]
    oab_ref[...] = _dot(hb, whi) + _dot(hlo, whi) + _dot(hb, wab_lo_ref[...])


def _inproj_even(x2d, gain, w_bf16, wab, conv_qkv, conv_sc, seqlen, *, tm=TM):
    t, d = x2d.shape
    n = w_bf16.shape[1]
    w4 = DN_WIDTH
    wab_hi = wab.astype(BF16)
    wab_lo = (wab - wab_hi.astype(F32)).astype(BF16)
    const = lambda shape: pl.BlockSpec(shape, lambda i: (0,) * len(shape))
    return pl.pallas_call(
        functools.partial(_inproj_even_kernel, tm=tm, tiles_per_seq=seqlen // tm),
        grid=(t // tm,),
        in_specs=[
            pl.BlockSpec((tm, d), lambda i: (i, 0)),
            const((1, d)), const((d, n)), const((d, LANES)), const((d, LANES)),
            const((DN_CONV, 3 * w4)), const((SC_CONV, SC_WIDTH)),
        ],
        out_specs=[pl.BlockSpec((tm, 5 * w4), lambda i: (i, 0)),
                   pl.BlockSpec((tm, LANES), lambda i: (i, 0))],
        out_shape=[jax.ShapeDtypeStruct((t, 5 * w4), BF16),
                   jax.ShapeDtypeStruct((t, LANES), F32)],
        scratch_shapes=[pltpu.VMEM((tm + SUBLANES, 3 * w4 + SC_WIDTH), F32)],
        compiler_params=_params(1),
        name="inproj_even",
    )(x2d, gain.reshape(1, d), w_bf16, wab_hi, wab_lo, conv_qkv, conv_sc)


def _inproj_odd_kernel(x_ref, g_ref, w_ref, qw_ref, kw_ref, bd_ref, o_ref, y_scr, ms_scr):
    nq = SWA_Q_HEADS * SWA_HEAD_DIM
    n_norm = nq // LANES + 1
    hb = _rms_rows(x_ref[...], g_ref[...]).astype(BF16)
    bd = bd_ref[...]
    for c in range(n_norm + 1):
        cs = slice(c * LANES, (c + 1) * LANES)
        y_scr[:, cs] = _dot(hb, w_ref[:, cs])
    for c in range(n_norm):
        cs = slice(c * LANES, (c + 1) * LANES)
        y = y_scr[:, cs]
        hi, lo = _split2(y * y)
        ms_scr[:, cs] = _dot(hi, bd) + _dot(lo, bd)
    for c in range(n_norm):
        cs = slice(c * LANES, (c + 1) * LANES)
        inv = lax.rsqrt(ms_scr[:, cs] * (1.0 / SWA_HEAD_DIM) + RMS_EPS)
        if c < n_norm - 1:
            y = y_scr[:, cs] * inv * qw_ref[...] * (SWA_HEAD_DIM ** -0.5)
        else:
            y = y_scr[:, cs] * inv * kw_ref[...]
        o_ref[:, cs] = y.astype(o_ref.dtype)
    vs = slice(n_norm * LANES, (n_norm + 1) * LANES)
    o_ref[:, vs] = y_scr[:, vs].astype(o_ref.dtype)


def _inproj_odd(x2d, gain, w_bf16, q_norm, k_norm, *, tm=TM):
    t, d = x2d.shape
    n = w_bf16.shape[1]
    rep = LANES // SWA_HEAD_DIM
    ids = jnp.arange(LANES) // SWA_HEAD_DIM
    bd = (ids[:, None] == ids[None, :]).astype(BF16)
    const = lambda shape: pl.BlockSpec(shape, lambda i: (0,) * len(shape))
    return pl.pallas_call(
        _inproj_odd_kernel,
        grid=(t // tm,),
        in_specs=[pl.BlockSpec((tm, d), lambda i: (i, 0)), const((1, d)), const((d, n)),
                  const((1, LANES)), const((1, LANES)), const((LANES, LANES))],
        out_specs=pl.BlockSpec((tm, n), lambda i: (i, 0)),
        out_shape=jax.ShapeDtypeStruct((t, n), BF16),
        scratch_shapes=[pltpu.VMEM((tm, n), F32), pltpu.VMEM((tm, n - LANES), F32)],
        compiler_params=_params(1),
        name="inproj_odd",
    )(x2d, gain.reshape(1, d), w_bf16, jnp.tile(q_norm, rep).reshape(1, LANES),
      jnp.tile(k_norm, rep).reshape(1, LANES), bd)


def _outproj_kernel(*refs, n_parts):
    a_refs, w_ref, r_ref, o_ref = refs[:n_parts], refs[n_parts], refs[n_parts + 1], refs[-1]
    acc = r_ref[...]
    k0 = 0
    for a_ref in a_refs:
        kw = a_ref.shape[1]
        acc = acc + _dot(a_ref[...], w_ref[k0:k0 + kw, :])
        k0 += kw
    o_ref[...] = acc


def _outproj(parts, w_bf16, res, *, tm=TM):
    t, n = res.shape
    k = w_bf16.shape[0]
    in_specs = [pl.BlockSpec((tm, bw), functools.partial(lambda i, bi: (i, bi), bi=bi))
                for _, bw, bi in parts]
    in_specs += [pl.BlockSpec((k, n), lambda i: (0, 0)), pl.BlockSpec((tm, n), lambda i: (i, 0))]
    return pl.pallas_call(
        functools.partial(_outproj_kernel, n_parts=len(parts)),
        grid=(t // tm,),
        in_specs=in_specs,
        out_specs=pl.BlockSpec((tm, n), lambda i: (i, 0)),
        out_shape=jax.ShapeDtypeStruct((t, n), F32),
        compiler_params=_params(1),
        name="outproj",
    )(*[a for a, _, _ in parts], w_bf16, res)


def _deltanet_kernel(q_ref, k_ref, v_ref, sz_ref, ab_ref, gvec_ref, onorm_ref, lvlm_ref, o_ref,
                     gb_scr, oh_scr, s_scr, tinv_scr, lm_scr, x_scr, at_scr, cat_scr, kdt_scr,
                     qd_scr, ge_scr, uw_scr, mm_scr, bb_scr, qp_scr, op_scr, *, ts):
    n_pairs = ts // PAIR
    n_prob = n_pairs * DN_HEADS

    @pl.when(pl.program_id(1) == 0)
    def _():
        s_scr[...] = jnp.zeros_like(s_scr)

    ab = ab_ref[...]
    lane_t = lax.broadcasted_iota(jnp.int32, ab.shape, 1)
    z = ab + gvec_ref[0:1, :]
    softplus = jnp.maximum(z, 0.0) + jnp.log1p(jnp.exp(-jnp.abs(z)))
    g = -jnp.exp(gvec_ref[1:2, :]) * softplus
    gb_scr[...] = jnp.where(lane_t < DN_HEADS, g, 1.0 / (1.0 + jnp.exp(-ab)))

    lane = lax.broadcasted_iota(jnp.int32, (PAIR, LANES), 1)
    row = lax.broadcasted_iota(jnp.int32, (PAIR, LANES), 0)
    same = (row // DN_CHUNK) == (lane // DN_CHUNK)
    incl = same & (row >= lane)
    strict = same & (row > lane)
    cs_mask = incl.astype(BF16)
    eye = (row == lane).astype(F32)
    lvl1 = ((row // 2) == (lane // 2)) & (row > lane)
    first_chunk = row < DN_CHUNK
    first_chunk_col = lane < DN_CHUNK

    for p in range(n_pairs):
        rows = slice(p * PAIR, (p + 1) * PAIR)
        gbp = gb_scr[rows, :]
        gcc = _dot_mask_l(cs_mask, gbp)
        for h in range(DN_HEADS):
            i = p * DN_HEADS + h
            hc = slice(h * DN_HEAD_DIM, (h + 1) * DN_HEAD_DIM)
            gc = jnp.broadcast_to(gcc[:, h:h + 1], (PAIR, LANES))
            beta = jnp.broadcast_to(gbp[:, DN_HEADS + h:DN_HEADS + h + 1], (PAIR, LANES))
            dec = jnp.exp(jnp.where(incl, gc - gc.T, NEG_INF))
            egc = jnp.exp(gc)
            gl0 = gc[DN_CHUNK - 1:DN_CHUNK, :]
            gl1 = gc[PAIR - 1:PAIR, :]
            ekd = jnp.exp(jnp.where(first_chunk, gl0, gl1) - gc)
            ge_scr[i, 0:1, :] = jnp.exp(gl0)
            ge_scr[i, 1:2, :] = jnp.exp(gl1)
            qn = q_ref[rows, hc].astype(F32)
            kn_b = k_ref[rows, hc]
            kn = kn_b.astype(F32)
            kb = kn * beta
            both = _dot_nt(jnp.concatenate([kb.astype(BF16), q_ref[rows, hc]], axis=0), kn_b)
            lmat = jnp.where(strict, both[0:PAIR] * dec, 0.0)
            tinv_scr[i] = eye - jnp.where(lvl1, lmat, 0.0)
            lm_scr[i] = lmat.astype(BF16)
            at_scr[i] = (both[PAIR:2 * PAIR] * dec).astype(BF16)
            cat_scr[i, :, 0:LANES] = (kb * egc).astype(BF16)
            cat_scr[i, :, LANES:2 * LANES] = (v_ref[rows, hc].astype(F32) * beta).astype(BF16)
            qd_scr[i] = qn * egc
            kdt = (kn * ekd).T
            kdt_scr[i, 0] = jnp.where(first_chunk_col, kdt, 0.0).astype(BF16)
            kdt_scr[i, 1] = jnp.where(first_chunk_col, 0.0, kdt).astype(BF16)

    for lvl in range(lvlm_ref.shape[0]):
        for i in range(n_prob):
            x = _dot(tinv_scr[i].astype(BF16), lm_scr[i] * lvlm_ref[lvl])
            x_scr[i] = x.astype(BF16)
        for i in range(n_prob):
            t = tinv_scr[i]
            tinv_scr[i] = t - _dot(x_scr[i], t.astype(BF16))

    for i in range(n_prob):
        uw_scr[i] = _dot(tinv_scr[i].astype(BF16), cat_scr[i]).astype(BF16)
    for i in range(n_prob):
        uw = uw_scr[i]
        for c in range(2):
            r = _dot(kdt_scr[i, c], uw)
            mm_scr[i, c] = r[:, 0:LANES].astype(BF16)
            bb_scr[i, c] = r[:, LANES:2 * LANES]
        r = _dot(at_scr[i], uw)
        qp_scr[i] = (qd_scr[i] - r[:, 0:LANES]).astype(BF16)
        op_scr[i] = r[:, LANES:2 * LANES]

    state = [s_scr[h] for h in range(DN_HEADS)]
    for p in range(n_pairs):
        for h in range(DN_HEADS):
            i = p * DN_HEADS + h
            hc = slice(h * DN_HEAD_DIM, (h + 1) * DN_HEAD_DIM)
            st = state[h]
            for c in range(2):
                cr = slice(c * DN_CHUNK, (c + 1) * DN_CHUNK)
                sb = st.astype(BF16)
                oh_scr[p * PAIR + c * DN_CHUNK:p * PAIR + (c + 1) * DN_CHUNK, hc] = (
                    _dot(qp_scr[i, cr, :], sb) + op_scr[i, cr, :])
                st = st * ge_scr[i, c:c + 1, :] - _dot(mm_scr[i, c], sb) + bb_scr[i, c]
            state[h] = st
    for h in range(DN_HEADS):
        s_scr[h] = state[h]

    for h in range(DN_HEADS):
        hc = slice(h * DN_HEAD_DIM, (h + 1) * DN_HEAD_DIM)
        o = _rms_rows(oh_scr[:, hc], onorm_ref[...])
        o_ref[:, hc] = (o * sz_ref[:, hc].astype(F32)).astype(o_ref.dtype)


def _deltanet(proj, ab, a_log, dt_bias, o_norm, *, ts=TM):
    b, s, _ = proj.shape
    w4 = DN_WIDTH
    n_prob = (ts // PAIR) * DN_HEADS
    gvec = jnp.zeros((SUBLANES, LANES), F32)
    gvec = gvec.at[0, :DN_HEADS].set(dt_bias).at[1, :DN_HEADS].set(a_log)
    r = jnp.arange(PAIR)[:, None]
    c = jnp.arange(LANES)[None, :]
    lvl_masks = []
    blk = 2
    while blk < DN_CHUNK:
        lvl_masks.append((r // (2 * blk) == c // (2 * blk)) & (r // blk != c // blk) & (r > c))
        blk *= 2
    lvlm = jnp.stack(lvl_masks).astype(BF16)
    nl = lvlm.shape[0]
    blk_f32 = pltpu.VMEM((n_prob, PAIR, LANES), F32)
    blk_b16 = pltpu.VMEM((n_prob, PAIR, LANES), BF16)
    col = lambda cb: pl.BlockSpec((None, ts, w4), functools.partial(
        lambda i, j, cb: (i, j, cb), cb=cb))
    return pl.pallas_call(
        functools.partial(_deltanet_kernel, ts=ts),
        grid=(b, s // ts),
        in_specs=[
            col(0), col(1), col(2), col(3),
            pl.BlockSpec((None, ts, LANES), lambda i, j: (i, j, 0)),
            pl.BlockSpec((SUBLANES, LANES), lambda i, j: (0, 0)),
            pl.BlockSpec((1, DN_HEAD_DIM), lambda i, j: (0, 0)),
            pl.BlockSpec((nl, PAIR, LANES), lambda i, j: (0, 0, 0)),
        ],
        out_specs=pl.BlockSpec((None, ts, w4), lambda i, j: (i, j, 0)),
        out_shape=jax.ShapeDtypeStruct((b, s, w4), BF16),
        scratch_shapes=[
            pltpu.VMEM((ts, LANES), F32),
            pltpu.VMEM((ts, w4), F32),
            pltpu.VMEM((DN_HEADS, DN_HEAD_DIM, DN_HEAD_DIM), F32),
            blk_f32, blk_b16, blk_b16, blk_b16,
            pltpu.VMEM((n_prob, PAIR, 2 * LANES), BF16),
            pltpu.VMEM((n_prob, 2, DN_HEAD_DIM, PAIR), BF16),
            blk_f32,
            pltpu.VMEM((n_prob, SUBLANES, LANES), F32),
            pltpu.VMEM((n_prob, PAIR, 2 * LANES), BF16),
            pltpu.VMEM((n_prob, 2, DN_HEAD_DIM, DN_HEAD_DIM), BF16),
            pltpu.VMEM((n_prob, 2, DN_HEAD_DIM, DN_HEAD_DIM), F32),
            blk_b16, blk_f32,
        ],
        compiler_params=_params(2),
        name="deltanet",
    )(proj, proj, proj, proj, ab, gvec, o_norm.reshape(1, -1), lvlm)


def _swa_kernel(q_ref, kv_ref, halo_ref, sink_ref, o_ref, kx_scr, vx_scr, s_scr, p_scr, *, tq):
    s_idx = pl.program_id(1)
    blk = SWA_WINDOW
    hd = SWA_HEAD_DIM

    lane = lax.broadcasted_iota(jnp.int32, (blk + tq, LANES), 1)
    lo_half = lane < hd
    live = (s_idx > 0).astype(F32)
    for src, scr in ((0, kx_scr), (1, vx_scr)):
        cols = slice(src * LANES, (src + 1) * LANES)
        win = jnp.concatenate([halo_ref[:, cols].astype(F32) * live, kv_ref[:, cols].astype(F32)],
                              axis=0)
        swp = pltpu.roll(win, hd, axis=1)
        scr[0] = jnp.where(lo_half, win, 0.0).astype(BF16)
        scr[1] = jnp.where(lo_half, 0.0, swp).astype(BF16)
        scr[2] = jnp.where(lo_half, swp, 0.0).astype(BF16)
        scr[3] = jnp.where(lo_half, 0.0, win).astype(BF16)

    qi = lax.broadcasted_iota(jnp.int32, (blk, 2 * blk), 0)
    kj = lax.broadcasted_iota(jnp.int32, (blk, 2 * blk), 1)
    band = (kj > qi) & (kj <= qi + blk)

    def block_body(j, carry):
        r0 = pl.multiple_of(j * blk, blk)
        qrows = pl.ds(r0, blk)
        wrows = pl.ds(r0, 2 * blk)
        valid = band & ((kj >= blk) | (s_idx > 0) | (j > 0))
        for head in range(SWA_Q_HEADS):
            var = 2 * (head // SWA_GROUP) + head % 2
            cs = slice((head // 2) * LANES, (head // 2 + 1) * LANES)
            s_scr[head] = jnp.where(valid, _dot_nt(q_ref[qrows, cs], kx_scr[var, wrows, :]),
                                    NEG_INF)
        for head in range(SWA_Q_HEADS):
            sink = sink_ref[head]
            sc = s_scr[head]
            m = jnp.maximum(jnp.max(sc, axis=-1, keepdims=True), sink)
            pexp = jnp.exp(sc - m)
            den = jnp.sum(pexp, axis=-1, keepdims=True) + jnp.exp(sink - m)
            p_scr[head] = (pexp * (1.0 / den)).astype(BF16)
        for cc in range(SWA_Q_HEADS // 2):
            g = (2 * cc) // SWA_GROUP
            acc = (_dot(p_scr[2 * cc], vx_scr[2 * g, wrows, :])
                   + _dot(p_scr[2 * cc + 1], vx_scr[2 * g + 1, wrows, :]))
            o_ref[qrows, cc * LANES:(cc + 1) * LANES] = acc.astype(o_ref.dtype)
        return carry

    lax.fori_loop(0, tq // blk, block_body, 0)


def _swa(qkv, sinks, *, tq=TM):
    b, s, _ = qkv.shape
    nq = SWA_Q_HEADS * SWA_HEAD_DIM
    nkv = 2 * SWA_KV_HEADS * SWA_HEAD_DIM
    blk = SWA_WINDOW
    return pl.pallas_call(
        functools.partial(_swa_kernel, tq=tq),
        grid=(b, s // tq),
        in_specs=[
            pl.BlockSpec((None, tq, nq), lambda i, j: (i, j, 0)),
            pl.BlockSpec((None, tq, nkv), lambda i, j: (i, j, nq // nkv)),
            pl.BlockSpec((None, blk, nkv),
                         lambda i, j: (i, jnp.maximum(j * (tq // blk) - 1, 0), nq // nkv)),
            pl.BlockSpec(memory_space=pltpu.SMEM),
        ],
        out_specs=pl.BlockSpec((None, tq, nq), lambda i, j: (i, j, 0)),
        out_shape=jax.ShapeDtypeStruct((b, s, nq), BF16),
        scratch_shapes=[
            pltpu.VMEM((2 * SWA_KV_HEADS, blk + tq, LANES), BF16),
            pltpu.VMEM((2 * SWA_KV_HEADS, blk + tq, LANES), BF16),
            pltpu.VMEM((SWA_Q_HEADS, blk, 2 * blk), F32),
            pltpu.VMEM((SWA_Q_HEADS, blk, 2 * blk), BF16),
        ],
        compiler_params=_params(2),
        name="swa",
    )(qkv, qkv, qkv, sinks)


def _ffn_kernel(te_ref, nused_ref, x_ref, g_ref, w1_ref, w3_ref, w2_ref, o_ref, h_scr, acc_scr,
                *, add_residual, nf):
    i = pl.program_id(0)
    f = pl.program_id(1)
    used = i < nused_ref[0]

    @pl.when(used & (f == 0))
    def _():
        h_scr[...] = _rms_rows(x_ref[...], g_ref[...]).astype(BF16)
        acc_scr[...] = jnp.zeros_like(acc_scr)

    @pl.when(used)
    def _():
        hb = h_scr[...]
        a = _dot(hb, w1_ref[...])
        b = _dot(hb, w3_ref[...])
        acc_scr[...] += _dot((_silu(a) * b).astype(BF16), w2_ref[...])

    @pl.when(f == nf - 1)
    def _():
        @pl.when(used)
        def _():
            y = acc_scr[...]
            o_ref[...] = (x_ref[...] + y) if add_residual else y

        @pl.when(jnp.logical_not(used))
        def _():
            o_ref[...] = jnp.zeros_like(o_ref)


def _ffn(tile_expert, n_used, x2d, gain, w1, w3, w2, *, add_residual, tm=TM_FFN, tf=TF):
    r, d = x2d.shape
    nf = w1.shape[2] // tf

    def fsel(i, f, nused):
        return jnp.where(i < nused[0], f, 0)

    grid_spec = pltpu.PrefetchScalarGridSpec(
        num_scalar_prefetch=2,
        grid=(r // tm, nf),
        in_specs=[
            pl.BlockSpec((tm, d), lambda i, f, te, nu: (i, 0)),
            pl.BlockSpec((1, d), lambda i, f, te, nu: (0, 0)),
            pl.BlockSpec((None, d, tf), lambda i, f, te, nu: (te[i], 0, fsel(i, f, nu))),
            pl.BlockSpec((None, d, tf), lambda i, f, te, nu: (te[i], 0, fsel(i, f, nu))),
            pl.BlockSpec((None, tf, d), lambda i, f, te, nu: (te[i], fsel(i, f, nu), 0)),
        ],
        out_specs=pl.BlockSpec((tm, d), lambda i, f, te, nu: (i, 0)),
        scratch_shapes=[pltpu.VMEM((tm, d), BF16), pltpu.VMEM((tm, d), F32)],
    )
    return pl.pallas_call(
        functools.partial(_ffn_kernel, add_residual=add_residual, nf=nf),
        grid_spec=grid_spec,
        out_shape=jax.ShapeDtypeStruct((r, d), F32),
        compiler_params=_params(2),
        name="ffn_res" if add_residual else "ffn_experts",
    )(tile_expert, n_used, x2d, gain.reshape(1, d), w1, w3, w2)


def _router_kernel(x_ref, g_ref, whi_ref, wlo_ref, tri_ref, info_ref, cnt_ref, carry_scr):
    @pl.when(pl.program_id(0) == 0)
    def _():
        carry_scr[...] = jnp.zeros_like(carry_scr)

    h = _rms_rows(x_ref[...], g_ref[...])
    hhi, hlo = _split2(h)
    whi = whi_ref[...]
    logits = _dot(hhi, whi) + _dot(hlo, whi) + _dot(hhi, wlo_ref[...])
    lane = lax.broadcasted_iota(jnp.int32, logits.shape, 1).astype(F32)
    lg = jnp.where(lane < N_EXPERTS, logits, -jnp.inf)
    v1 = jnp.max(lg, axis=-1, keepdims=True)
    i1 = jnp.min(jnp.where(lg == v1, lane, float(LANES)), axis=-1, keepdims=True)
    lg2 = jnp.where(lane == i1, -jnp.inf, lg)
    v2 = jnp.max(lg2, axis=-1, keepdims=True)
    i2 = jnp.min(jnp.where(lg2 == v2, lane, float(LANES)), axis=-1, keepdims=True)
    e = jnp.exp(v2 - v1)
    gate1 = 1.0 / (1.0 + e)
    gate2 = e / (1.0 + e)
    m1 = lane == i1
    m2 = lane == i2
    member = (m1 | m2).astype(F32)
    carry = carry_scr[0:1, :]
    rank = _dot(tri_ref[...], member.astype(BF16)) + carry
    r1 = jnp.sum(jnp.where(m1, rank, 0.0), axis=-1, keepdims=True)
    r2 = jnp.sum(jnp.where(m2, rank, 0.0), axis=-1, keepdims=True)
    total = carry + jnp.sum(member, axis=0, keepdims=True)
    carry_scr[...] = jnp.broadcast_to(total, carry_scr.shape)
    cnt_ref[...] = jnp.broadcast_to(total, cnt_ref.shape)
    info = jnp.where(lane == 0, i1, 0.0)
    info = jnp.where(lane == 1, i2, info)
    info = jnp.where(lane == 2, gate1, info)
    info = jnp.where(lane == 3, gate2, info)
    info = jnp.where(lane == 4, r1, info)
    info = jnp.where(lane == 5, r2, info)
    info_ref[...] = info


def _router(x2d, gain, w_router, *, tm=TM):
    t, d = x2d.shape
    wpad = jnp.zeros((d, LANES), F32).at[:, :N_EXPERTS].set(w_router)
    whi = wpad.astype(BF16)
    wlo = (wpad - whi.astype(F32)).astype(BF16)
    tri = (jnp.arange(tm)[:, None] > jnp.arange(tm)[None, :]).astype(BF16)
    const = lambda shape: pl.BlockSpec(shape, lambda i: (0,) * len(shape))
    return pl.pallas_call(
        _router_kernel,
        grid=(t // tm,),
        in_specs=[pl.BlockSpec((tm, d), lambda i: (i, 0)), const((1, d)), const((d, LANES)),
                  const((d, LANES)), const((tm, tm))],
        out_specs=[pl.BlockSpec((tm, LANES), lambda i: (i, 0)), const((SUBLANES, LANES))],
        out_shape=[jax.ShapeDtypeStruct((t, LANES), F32),
                   jax.ShapeDtypeStruct((SUBLANES, LANES), F32)],
        scratch_shapes=[pltpu.VMEM((SUBLANES, LANES), F32)],
        compiler_params=_params(1),
        name="router",
    )(x2d, gain.reshape(1, d), whi, wlo, tri)


def _for_each_row(n_rows, fn):
    def body(g, c):
        for u in range(COPY_UNROLL):
            fn(g * COPY_UNROLL + u)
        return c
    lax.fori_loop(0, n_rows // COPY_UNROLL, body, 0)


def _dispatch_kernel(pad_start_ref, pad_cnt_ref, pos_ref, x_ref, xs_ref, zero_scr, sem, *, tm):
    @pl.when(pl.program_id(0) == 0)
    def _():
        zero_scr[...] = jnp.zeros_like(zero_scr)

        def pad_copy(e, r):
            return pltpu.make_async_copy(zero_scr.at[pl.ds(0, 1)],
                                         xs_ref.at[pl.ds(pad_start_ref[e] + r, 1)], sem)

        def tail_copy(j):
            dst = pl.multiple_of(pad_start_ref[N_EXPERTS] + j * tm, tm)
            return pltpu.make_async_copy(zero_scr, xs_ref.at[pl.ds(dst, tm)], sem)

        for e in range(N_EXPERTS):
            lax.fori_loop(0, pad_cnt_ref[e], lambda r, c: (pad_copy(e, r).start(), c)[1], 0)
        lax.fori_loop(0, pad_cnt_ref[N_EXPERTS], lambda j, c: (tail_copy(j).start(), c)[1], 0)
        for e in range(N_EXPERTS):
            lax.fori_loop(0, pad_cnt_ref[e], lambda r, c: (pad_copy(e, r).wait(), c)[1], 0)
        lax.fori_loop(0, pad_cnt_ref[N_EXPERTS], lambda j, c: (tail_copy(j).wait(), c)[1], 0)

    def copy(r, k):
        return pltpu.make_async_copy(x_ref.at[pl.ds(r, 1)],
                                     xs_ref.at[pl.ds(pos_ref[0, 0, 2 * r + k], 1)], sem)

    def start(r):
        for k in range(2):
            copy(r, k).start(priority=k)

    def wait(r):
        for k in range(2):
            copy(r, k).wait()

    _for_each_row(tm, start)
    _for_each_row(tm, wait)


def _dispatch(pos, pad_start, pad_cnt, x2d, n_rows, *, tm=TD):
    t, d = x2d.shape
    pos3 = pos.reshape(t // tm, 1, 2 * tm)
    grid_spec = pltpu.PrefetchScalarGridSpec(
        num_scalar_prefetch=2,
        grid=(t // tm,),
        in_specs=[
            pl.BlockSpec((1, 1, 2 * tm), lambda i, ps, pc: (i, 0, 0), memory_space=pltpu.SMEM),
            pl.BlockSpec((tm, d), lambda i, ps, pc: (i, 0)),
        ],
        out_specs=pl.BlockSpec(memory_space=pl.ANY),
        scratch_shapes=[pltpu.VMEM((tm, d), F32), pltpu.SemaphoreType.DMA],
    )
    return pl.pallas_call(
        functools.partial(_dispatch_kernel, tm=tm),
        grid_spec=grid_spec,
        out_shape=jax.ShapeDtypeStruct((n_rows, d), F32),
        compiler_params=_params(1),
        name="dispatch",
    )(pad_start, pad_cnt, pos3, x2d)


def _combine_kernel(pos_ref, x_ref, gate_ref, y_ref, o_ref, buf, sem, *, tm):
    def copy(r, k):
        return pltpu.make_async_copy(y_ref.at[pl.ds(pos_ref[0, 0, 2 * r + k], 1)],
                                     buf.at[k, pl.ds(r, 1)], sem)

    def start(r):
        for k in range(2):
            copy(r, k).start(priority=k)

    def wait(r):
        for k in range(2):
            copy(r, k).wait()

    _for_each_row(tm, start)
    _for_each_row(tm, wait)
    g = gate_ref[...]
    o_ref[...] = x_ref[...] + g[:, 2:3] * buf[0] + g[:, 3:4] * buf[1]


def _combine(pos, x2d, info, y, *, tm=TD):
    t, d = x2d.shape
    pos3 = pos.reshape(t // tm, 1, 2 * tm)
    return pl.pallas_call(
        functools.partial(_combine_kernel, tm=tm),
        grid=(t // tm,),
        in_specs=[
            pl.BlockSpec((1, 1, 2 * tm), lambda i: (i, 0, 0), memory_space=pltpu.SMEM),
            pl.BlockSpec((tm, d), lambda i: (i, 0)),
            pl.BlockSpec((tm, LANES), lambda i: (i, 0)),
            pl.BlockSpec(memory_space=pl.ANY),
        ],
        out_specs=pl.BlockSpec((tm, d), lambda i: (i, 0)),
        out_shape=jax.ShapeDtypeStruct((t, d), F32),
        scratch_shapes=[pltpu.VMEM((2, tm, d), F32), pltpu.SemaphoreType.DMA],
        compiler_params=_params(1),
        name="combine",
    )(pos3, x2d, info, y)


def _even_layer(x2d, bsz, seqlen, norm1, w_in, conv_qkv, a_log, dt_bias, o_norm, conv_sc, w_out,
                norm2, w1, w3, w2):
    w4 = DN_WIDTH
    c_ab = 4 * w4
    c_sc = c_ab + 2 * DN_HEADS
    w_main = jnp.concatenate([w_in[:, :c_ab], w_in[:, c_sc:]], axis=1).astype(BF16)
    w_ab = jnp.zeros((D_MODEL, LANES), F32).at[:, :2 * DN_HEADS].set(w_in[:, c_ab:c_sc])
    proj, ab = _inproj_even(x2d, norm1, w_main, w_ab, conv_qkv, conv_sc, seqlen)
    o = _deltanet(proj.reshape(bsz, seqlen, -1), ab.reshape(bsz, seqlen, LANES),
                  a_log, dt_bias, o_norm)
    x2d = _outproj([(o.reshape(bsz * seqlen, w4), w4, 0), (proj, SC_WIDTH, 4)],
                   w_out.astype(BF16), x2d)
    n_tiles = x2d.shape[0] // TM_FFN
    te = jnp.zeros((n_tiles,), jnp.int32)
    nused = jnp.full((1,), n_tiles, jnp.int32)
    return _ffn(te, nused, x2d, norm2, w1[None].astype(BF16), w3[None].astype(BF16),
                w2[None].astype(BF16), add_residual=True)


def _odd_layer(x2d, bsz, seqlen, norm1, w_in, q_norm, k_norm, sinks, w_out, norm2, w_router,
               w1, w3, w2, *, tm=TM_FFN):
    t = x2d.shape[0]
    nq = SWA_Q_HEADS * SWA_HEAD_DIM
    qkv = _inproj_odd(x2d, norm1, w_in.astype(BF16), q_norm, k_norm)
    att = _swa(qkv.reshape(bsz, seqlen, -1), sinks)
    x2d = _outproj([(att.reshape(t, nq), nq, 0)], w_out.astype(BF16), x2d)

    info, cnt = _router(x2d, norm2, w_router)
    counts = cnt[0, :N_EXPERTS].astype(jnp.int32)
    padded = ((counts + tm - 1) // tm) * tm
    ends = jnp.cumsum(padded)
    offsets = ends - padded
    n_rows = 2 * t + N_EXPERTS * tm
    n_tiles = n_rows // tm
    tile_start = jnp.arange(n_tiles, dtype=jnp.int32) * tm
    tile_expert = jnp.minimum(jnp.sum(tile_start[:, None] >= ends[None, :], axis=1),
                              N_EXPERTS - 1).astype(jnp.int32)
    n_used = (ends[-1:] // tm).astype(jnp.int32)
    idx = info[:, 0:2].astype(jnp.int32)
    pos = (offsets[idx] + info[:, 4:6].astype(jnp.int32)).reshape(-1)

    pad_start = jnp.concatenate([offsets + counts, ends[-1:]]).astype(jnp.int32)
    pad_cnt = jnp.concatenate([padded - counts, (n_rows - ends[-1:]) // TD]).astype(jnp.int32)
    xs = _dispatch(pos, pad_start, pad_cnt, x2d, n_rows)
    ys = _ffn(tile_expert, n_used, xs, norm2, w1.astype(BF16), w3.astype(BF16), w2.astype(BF16),
              add_residual=False, tm=tm)
    return _combine(pos, x2d, info, ys)


def kernel(x, ev_norm1, ev_w_in, ev_conv_qkv, ev_a_log, ev_dt_bias, ev_o_norm, ev_conv_sc,
           ev_w_out, ev_norm2, ev_ffn_w1, ev_ffn_w3, ev_ffn_w2, od_norm1, od_w_in, od_q_norm,
           od_k_norm, od_sinks, od_w_out, od_norm2, od_router, od_moe_w1, od_moe_w3, od_moe_w2):
    bsz, seqlen, d = x.shape
    x2d = x.reshape(bsz * seqlen, d)
    depth = ev_norm1.shape[0] + od_norm1.shape[0]
    for layer in range(depth):
        i = layer // 2
        if layer % 2 == 0:
            x2d = _even_layer(x2d, bsz, seqlen, ev_norm1[i], ev_w_in[i], ev_conv_qkv[i],
                              ev_a_log[i], ev_dt_bias[i], ev_o_norm[i], ev_conv_sc[i],
                              ev_w_out[i], ev_norm2[i], ev_ffn_w1[i], ev_ffn_w3[i], ev_ffn_w2[i])
        else:
            x2d = _odd_layer(x2d, bsz, seqlen, od_norm1[i], od_w_in[i], od_q_norm[i],
                             od_k_norm[i], od_sinks[i], od_w_out[i], od_norm2[i], od_router[i],
                             od_moe_w1[i], od_moe_w3[i], od_moe_w2[i])
    return x2d.reshape(bsz, seqlen, d)
```

```python
import functools

import jax
import jax.numpy as jnp
from jax import lax
from jax.experimental import pallas as pl
from jax.experimental.pallas import tpu as pltpu

F32 = jnp.float32
BF16 = jnp.bfloat16

D_MODEL = 1024
RMS_EPS = 1e-6
L2_EPS = 1e-6
NEG_INF = -1e30

DN_HEADS = 4
DN_HEAD_DIM = 128
DN_WIDTH = DN_HEADS * DN_HEAD_DIM
DN_CONV = 4
DN_CHUNK = 64
SC_WIDTH = 512
SC_CONV = 3

SWA_Q_HEADS = 16
SWA_KV_HEADS = 2
SWA_HEAD_DIM = 64
SWA_WINDOW = 128
SWA_GROUP = SWA_Q_HEADS // SWA_KV_HEADS

N_EXPERTS = 8

LANES = 128
SUBLANES = 8
PAIR = 2 * DN_CHUNK

TM = 512
TM_FFN = 1024
TF = 512
TD = 256
COPY_UNROLL = 8
VMEM_LIMIT = 56 * 1024 * 1024


def _dot(a, b):
    return jnp.dot(a, b, preferred_element_type=F32)


def _dot_nt(a, b):
    return lax.dot_general(a, b, (((1,), (1,)), ((), ())), preferred_element_type=F32)


def _split2(x):
    hi = x.astype(BF16)
    lo = (x - hi.astype(F32)).astype(BF16)
    return hi, lo


def _split3(x):
    hi = x.astype(BF16)
    r = x - hi.astype(F32)
    mid = r.astype(BF16)
    lo = (r - mid.astype(F32)).astype(BF16)
    return hi, mid, lo


def _dot_mask_l(mask_bf16, x):
    h, m, l = _split3(x)
    return _dot(mask_bf16, h) + _dot(mask_bf16, m) + _dot(mask_bf16, l)


def _silu(x):
    return x * (1.0 / (1.0 + jnp.exp(-x)))


def _rms_rows(x, gain):
    ms = jnp.mean(x * x, axis=-1, keepdims=True)
    return x * lax.rsqrt(ms + RMS_EPS) * gain


def _params(n_axes):
    return pltpu.CompilerParams(dimension_semantics=("arbitrary",) * n_axes,
                                vmem_limit_bytes=VMEM_LIMIT)


def _inproj_even_kernel(x_ref, g_ref, w_ref, wab_hi_ref, wab_lo_ref, cq_ref, cs_ref,
                        o_ref, oab_ref, xe_scr, *, tm, tiles_per_seq):
    w4 = DN_WIDTH
    ncv = 3 * w4 + SC_WIDTH
    h = _rms_rows(x_ref[...], g_ref[...])
    hb = h.astype(BF16)

    def proj(part):
        return _dot(hb, w_ref[:, part * w4:(part + 1) * w4])

    @pl.when(pl.program_id(0) == 0)
    def _():
        xe_scr[tm:tm + SUBLANES, :] = jnp.zeros((SUBLANES, ncv), F32)

    live = pl.program_id(0) % tiles_per_seq > 0
    xe_scr[0:SUBLANES, :] = jnp.where(live, xe_scr[tm:tm + SUBLANES, :], 0.0)
    def put(part):
        xe_scr[SUBLANES:SUBLANES + tm, part * w4:(part + 1) * w4] = proj(part)

    def conv(cols, wc_ref, width):
        xe = xe_scr[:, cols]
        acc = None
        for j in range(width):
            back = width - 1 - j
            xs = xe if back == 0 else pltpu.roll(xe, back, axis=0)
            term = xs[SUBLANES:SUBLANES + tm] * wc_ref[j:j + 1, :]
            acc = term if acc is None else acc + term
        return acc

    def mixer_input(part):
        cols = slice(part * w4, (part + 1) * w4)
        y = _silu(conv(cols, cq_ref.at[:, cols], DN_CONV))
        if part == 2:
            o_ref[:, cols] = y.astype(o_ref.dtype)
            return
        for hh in range(DN_HEADS):
            hc = slice(hh * DN_HEAD_DIM, (hh + 1) * DN_HEAD_DIM)
            yh = y[:, hc]
            inv = lax.rsqrt(jnp.sum(yh * yh, axis=-1, keepdims=True) + L2_EPS)
            if part == 0:
                inv = inv * (DN_HEAD_DIM ** -0.5)
            o_ref[:, part * w4 + hh * DN_HEAD_DIM:part * w4 + (hh + 1) * DN_HEAD_DIM] = (
                yh * inv).astype(o_ref.dtype)

    put(0)
    put(1)
    mixer_input(0)
    put(2)
    mixer_input(1)
    xe_scr[SUBLANES:SUBLANES + tm, 3 * w4:ncv] = proj(5) * proj(6)
    mixer_input(2)
    bg = proj(4)
    ysc = bg * conv(slice(3 * w4, ncv), cs_ref, SC_CONV)
    o_ref[:, 4 * w4:5 * w4] = ysc.astype(o_ref.dtype)
    o_ref[:, 3 * w4:4 * w4] = _silu(proj(3)).astype(o_ref.dtype)

    hlo = (h - hb.astype(F32)).astype(BF16)
    whi = wab_hi_ref[...]
    oab_ref[...] = _dot(hb, whi) + _dot(hlo, whi) + _dot(hb, wab_lo_ref[...])


def _inproj_even(x2d, gain, w_bf16, wab, conv_qkv, conv_sc, seqlen, *, tm=TM):
    t, d = x2d.shape
    n = w_bf16.shape[1]
    w4 = DN_WIDTH
    wab_hi = wab.astype(BF16)
    wab_lo = (wab - wab_hi.astype(F32)).astype(BF16)
    const = lambda shape: pl.BlockSpec(shape, lambda i: (0,) * len(shape))
    return pl.pallas_call(
        functools.partial(_inproj_even_kernel, tm=tm, tiles_per_seq=seqlen // tm),
        grid=(t // tm,),
        in_specs=[
            pl.BlockSpec((tm, d), lambda i: (i, 0)),
            const((1, d)), const((d, n)), const((d, LANES)), const((d, LANES)),
            const((DN_CONV, 3 * w4)), const((SC_CONV, SC_WIDTH)),
        ],
        out_specs=[pl.BlockSpec((tm, 5 * w4), lambda i: (i, 0)),
                   pl.BlockSpec((tm, LANES), lambda i: (i, 0))],
        out_shape=[jax.ShapeDtypeStruct((t, 5 * w4), BF16),
                   jax.ShapeDtypeStruct((t, LANES), F32)],
        scratch_shapes=[pltpu.VMEM((tm + SUBLANES, 3 * w4 + SC_WIDTH), F32)],
        compiler_params=_params(1),
        name="inproj_even",
    )(x2d, gain.reshape(1, d), w_bf16, wab_hi, wab_lo, conv_qkv, conv_sc)


def _inproj_odd_kernel(x_ref, g_ref, w_ref, qw_ref, kw_ref, bd_ref, o_ref, y_scr, ms_scr):
    nq = SWA_Q_HEADS * SWA_HEAD_DIM
    n_norm = nq // LANES + 1
    hb = _rms_rows(x_ref[...], g_ref[...]).astype(BF16)
    bd = bd_ref[...]
    for c in range(n_norm + 1):
        cs = slice(c * LANES, (c + 1) * LANES)
        y_scr[:, cs] = _dot(hb, w_ref[:, cs])
    for c in range(n_norm):
        cs = slice(c * LANES, (c + 1) * LANES)
        y = y_scr[:, cs]
        hi, lo = _split2(y * y)
        ms_scr[:, cs] = _dot(hi, bd) + _dot(lo, bd)
    for c in range(n_norm):
        cs = slice(c * LANES, (c + 1) * LANES)
        inv = lax.rsqrt(ms_scr[:, cs] * (1.0 / SWA_HEAD_DIM) + RMS_EPS)
        if c < n_norm - 1:
            y = y_scr[:, cs] * inv * qw_ref[...] * (SWA_HEAD_DIM ** -0.5)
        else:
            y = y_scr[:, cs] * inv * kw_ref[...]
        o_ref[:, cs] = y.astype(o_ref.dtype)
    vs = slice(n_norm * LANES, (n_norm + 1) * LANES)
    o_ref[:, vs] = y_scr[:, vs].astype(o_ref.dtype)


def _inproj_odd(x2d, gain, w_bf16, q_norm, k_norm, *, tm=TM):
    t, d = x2d.shape
    n = w_bf16.shape[1]
    rep = LANES // SWA_HEAD_DIM
    ids = jnp.arange(LANES) // SWA_HEAD_DIM
    bd = (ids[:, None] == ids[None, :]).astype(BF16)
    const = lambda shape: pl.BlockSpec(shape, lambda i: (0,) * len(shape))
    return pl.pallas_call(
        _inproj_odd_kernel,
        grid=(t // tm,),
        in_specs=[pl.BlockSpec((tm, d), lambda i: (i, 0)), const((1, d)), const((d, n)),
                  const((1, LANES)), const((1, LANES)), const((LANES, LANES))],
        out_specs=pl.BlockSpec((tm, n), lambda i: (i, 0)),
        out_shape=jax.ShapeDtypeStruct((t, n), BF16),
        scratch_shapes=[pltpu.VMEM((tm, n), F32), pltpu.VMEM((tm, n - LANES), F32)],
        compiler_params=_params(1),
        name="inproj_odd",
    )(x2d, gain.reshape(1, d), w_bf16, jnp.tile(q_norm, rep).reshape(1, LANES),
      jnp.tile(k_norm, rep).reshape(1, LANES), bd)


def _outproj_kernel(*refs, n_parts):
    a_refs, w_ref, r_ref, o_ref = refs[:n_parts], refs[n_parts], refs[n_parts + 1], refs[-1]
    acc = r_ref[...]
    k0 = 0
    for a_ref in a_refs:
        kw = a_ref.shape[1]
        acc = acc + _dot(a_ref[...], w_ref[k0:k0 + kw, :])
        k0 += kw
    o_ref[...] = acc


def _outproj(parts, w_bf16, res, *, tm=TM):
    t, n = res.shape
    k = w_bf16.shape[0]
    in_specs = [pl.BlockSpec((tm, bw), functools.partial(lambda i, bi: (i, bi), bi=bi))
                for _, bw, bi in parts]
    in_specs += [pl.BlockSpec((k, n), lambda i: (0, 0)), pl.BlockSpec((tm, n), lambda i: (i, 0))]
    return pl.pallas_call(
        functools.partial(_outproj_kernel, n_parts=len(parts)),
        grid=(t // tm,),
        in_specs=in_specs,
        out_specs=pl.BlockSpec((tm, n), lambda i: (i, 0)),
        out_shape=jax.ShapeDtypeStruct((t, n), F32),
        compiler_params=_params(1),
        name="outproj",
    )(*[a for a, _, _ in parts], w_bf16, res)


def _deltanet_kernel(q_ref, k_ref, v_ref, sz_ref, ab_ref, gvec_ref, onorm_ref, lvlm_ref, o_ref,
                     gb_scr, oh_scr, s_scr, tinv_scr, lm_scr, x_scr, at_scr, cat_scr, kdt_scr,
                     qd_scr, ge_scr, uw_scr, mm_scr, bb_scr, qp_scr, op_scr, *, ts):
    n_pairs = ts // PAIR
    n_prob = n_pairs * DN_HEADS

    @pl.when(pl.program_id(1) == 0)
    def _():
        s_scr[...] = jnp.zeros_like(s_scr)

    ab = ab_ref[...]
    lane_t = lax.broadcasted_iota(jnp.int32, ab.shape, 1)
    z = ab + gvec_ref[0:1, :]
    softplus = jnp.maximum(z, 0.0) + jnp.log1p(jnp.exp(-jnp.abs(z)))
    g = -jnp.exp(gvec_ref[1:2, :]) * softplus
    gb_scr[...] = jnp.where(lane_t < DN_HEADS, g, 1.0 / (1.0 + jnp.exp(-ab)))

    lane = lax.broadcasted_iota(jnp.int32, (PAIR, LANES), 1)
    row = lax.broadcasted_iota(jnp.int32, (PAIR, LANES), 0)
    same = (row // DN_CHUNK) == (lane // DN_CHUNK)
    incl = same & (row >= lane)
    strict = same & (row > lane)
    cs_mask = incl.astype(BF16)
    eye = (row == lane).astype(F32)
    lvl1 = ((row // 2) == (lane // 2)) & (row > lane)
    first_chunk = row < DN_CHUNK
    first_chunk_col = lane < DN_CHUNK

    def stage1(p):
        rows = slice(p * PAIR, (p + 1) * PAIR)
        gbp = gb_scr[rows, :]
        gcc = _dot_mask_l(cs_mask, gbp)
        for h in range(DN_HEADS):
            i = p * DN_HEADS + h
            hc = slice(h * DN_HEAD_DIM, (h + 1) * DN_HEAD_DIM)
            gc = jnp.broadcast_to(gcc[:, h:h + 1], (PAIR, LANES))
            beta = jnp.broadcast_to(gbp[:, DN_HEADS + h:DN_HEADS + h + 1], (PAIR, LANES))
            dec = jnp.exp(jnp.where(incl, gc - gc.T, NEG_INF))
            egc = jnp.exp(gc)
            gl0 = gc[DN_CHUNK - 1:DN_CHUNK, :]
            gl1 = gc[PAIR - 1:PAIR, :]
            ekd = jnp.exp(jnp.where(first_chunk, gl0, gl1) - gc)
            ge_scr[i, 0:1, :] = jnp.exp(gl0)
            ge_scr[i, 1:2, :] = jnp.exp(gl1)
            qn = q_ref[rows, hc].astype(F32)
            kn_b = k_ref[rows, hc]
            kn = kn_b.astype(F32)
            kb = kn * beta
            both = _dot_nt(jnp.concatenate([kb.astype(BF16), q_ref[rows, hc]], axis=0), kn_b)
            lmat = jnp.where(strict, both[0:PAIR] * dec, 0.0)
            tinv_scr[i] = eye - jnp.where(lvl1, lmat, 0.0)
            lm_scr[i] = lmat.astype(BF16)
            at_scr[i] = (both[PAIR:2 * PAIR] * dec).astype(BF16)
            cat_scr[i, :, 0:LANES] = (kb * egc).astype(BF16)
            cat_scr[i, :, LANES:2 * LANES] = (v_ref[rows, hc].astype(F32) * beta).astype(BF16)
            qd_scr[i] = qn * egc
            kdt = (kn * ekd).T
            kdt_scr[i, 0] = jnp.where(first_chunk_col, kdt, 0.0).astype(BF16)
            kdt_scr[i, 1] = jnp.where(first_chunk_col, 0.0, kdt).astype(BF16)

    def stage2_left(lvl, probs):
        for i in probs:
            x = _dot(tinv_scr[i].astype(BF16), lm_scr[i] * lvlm_ref[lvl])
            x_scr[i] = x.astype(BF16)

    def stage2_right(lvl, probs):
        for i in probs:
            t = tinv_scr[i]
            tinv_scr[i] = t - _dot(x_scr[i], t.astype(BF16))

    def stage3_uw(probs):
        for i in probs:
            uw_scr[i] = _dot(tinv_scr[i].astype(BF16), cat_scr[i]).astype(BF16)

    def stage3_ops(i):
        uw = uw_scr[i]
        for c in range(2):
            r = _dot(kdt_scr[i, c], uw)
            mm_scr[i, c] = r[:, 0:LANES].astype(BF16)
            bb_scr[i, c] = r[:, LANES:2 * LANES]
        r = _dot(at_scr[i], uw)
        qp_scr[i] = (qd_scr[i] - r[:, 0:LANES]).astype(BF16)
        op_scr[i] = r[:, LANES:2 * LANES]

    state = [s_scr[h] for h in range(DN_HEADS)]

    def stage4(p):
        for h in range(DN_HEADS):
            i = p * DN_HEADS + h
            hc = slice(h * DN_HEAD_DIM, (h + 1) * DN_HEAD_DIM)
            st = state[h]
            for c in range(2):
                cr = slice(c * DN_CHUNK, (c + 1) * DN_CHUNK)
                sb = st.astype(BF16)
                oh_scr[p * PAIR + c * DN_CHUNK:p * PAIR + (c + 1) * DN_CHUNK, hc] = (
                    _dot(qp_scr[i, cr, :], sb) + op_scr[i, cr, :])
                st = st * ge_scr[i, c:c + 1, :] - _dot(mm_scr[i, c], sb) + bb_scr[i, c]
            state[h] = st

    probs = range(n_prob)
    for p in range(n_pairs):
        stage1(p)
    for lvl in range(lvlm_ref.shape[0]):
        stage2_left(lvl, probs)
        stage2_right(lvl, probs)
    stage3_uw(probs)
    for i in probs:
        stage3_ops(i)
    for p in range(n_pairs):
        stage4(p)
    for h in range(DN_HEADS):
        s_scr[h] = state[h]

    for h in range(DN_HEADS):
        hc = slice(h * DN_HEAD_DIM, (h + 1) * DN_HEAD_DIM)
        o = _rms_rows(oh_scr[:, hc], onorm_ref[...])
        o_ref[:, hc] = (o * sz_ref[:, hc].astype(F32)).astype(o_ref.dtype)


def _deltanet(proj, ab, a_log, dt_bias, o_norm, *, ts=TM):
    b, s, _ = proj.shape
    w4 = DN_WIDTH
    n_prob = (ts // PAIR) * DN_HEADS
    gvec = jnp.zeros((SUBLANES, LANES), F32)
    gvec = gvec.at[0, :DN_HEADS].set(dt_bias).at[1, :DN_HEADS].set(a_log)
    r = jnp.arange(PAIR)[:, None]
    c = jnp.arange(LANES)[None, :]
    lvl_masks = []
    blk = 2
    while blk < DN_CHUNK:
        lvl_masks.append((r // (2 * blk) == c // (2 * blk)) & (r // blk != c // blk) & (r > c))
        blk *= 2
    lvlm = jnp.stack(lvl_masks).astype(BF16)
    nl = lvlm.shape[0]
    blk_f32 = pltpu.VMEM((n_prob, PAIR, LANES), F32)
    blk_b16 = pltpu.VMEM((n_prob, PAIR, LANES), BF16)
    col = lambda cb: pl.BlockSpec((None, ts, w4), functools.partial(
        lambda i, j, cb: (i, j, cb), cb=cb))
    return pl.pallas_call(
        functools.partial(_deltanet_kernel, ts=ts),
        grid=(b, s // ts),
        in_specs=[
            col(0), col(1), col(2), col(3),
            pl.BlockSpec((None, ts, LANES), lambda i, j: (i, j, 0)),
            pl.BlockSpec((SUBLANES, LANES), lambda i, j: (0, 0)),
            pl.BlockSpec((1, DN_HEAD_DIM), lambda i, j: (0, 0)),
            pl.BlockSpec((nl, PAIR, LANES), lambda i, j: (0, 0, 0)),
        ],
        out_specs=pl.BlockSpec((None, ts, w4), lambda i, j: (i, j, 0)),
        out_shape=jax.ShapeDtypeStruct((b, s, w4), BF16),
        scratch_shapes=[
            pltpu.VMEM((ts, LANES), F32),
            pltpu.VMEM((ts, w4), F32),
            pltpu.VMEM((DN_HEADS, DN_HEAD_DIM, DN_HEAD_DIM), F32),
            blk_f32, blk_b16, blk_b16, blk_b16,
            pltpu.VMEM((n_prob, PAIR, 2 * LANES), BF16),
            pltpu.VMEM((n_prob, 2, DN_HEAD_DIM, PAIR), BF16),
            blk_f32,
            pltpu.VMEM((n_prob, SUBLANES, LANES), F32),
            pltpu.VMEM((n_prob, PAIR, 2 * LANES), BF16),
            pltpu.VMEM((n_prob, 2, DN_HEAD_DIM, DN_HEAD_DIM), BF16),
            pltpu.VMEM((n_prob, 2, DN_HEAD_DIM, DN_HEAD_DIM), F32),
            blk_b16, blk_f32,
        ],
        compiler_params=_params(2),
        name="deltanet",
    )(proj, proj, proj, proj, ab, gvec, o_norm.reshape(1, -1), lvlm)


def _swa_kernel(q_ref, kv_ref, halo_ref, sink_ref, o_ref, kx_scr, vx_scr, s_scr, p_scr, *, tq):
    s_idx = pl.program_id(1)
    blk = SWA_WINDOW
    hd = SWA_HEAD_DIM

    lane = lax.broadcasted_iota(jnp.int32, (blk + tq, LANES), 1)
    lo_half = lane < hd
    live = (s_idx > 0).astype(F32)
    for src, scr in ((0, kx_scr), (1, vx_scr)):
        cols = slice(src * LANES, (src + 1) * LANES)
        win = jnp.concatenate([halo_ref[:, cols].astype(F32) * live, kv_ref[:, cols].astype(F32)],
                              axis=0)
        swp = pltpu.roll(win, hd, axis=1)
        scr[0] = jnp.where(lo_half, win, 0.0).astype(BF16)
        scr[1] = jnp.where(lo_half, 0.0, swp).astype(BF16)
        scr[2] = jnp.where(lo_half, swp, 0.0).astype(BF16)
        scr[3] = jnp.where(lo_half, 0.0, win).astype(BF16)

    qi = lax.broadcasted_iota(jnp.int32, (blk, 2 * blk), 0)
    kj = lax.broadcasted_iota(jnp.int32, (blk, 2 * blk), 1)
    band = (kj > qi) & (kj <= qi + blk)

    def block_body(j, carry):
        r0 = pl.multiple_of(j * blk, blk)
        qrows = pl.ds(r0, blk)
        wrows = pl.ds(r0, 2 * blk)
        valid = band & ((kj >= blk) | (s_idx > 0) | (j > 0))
        for head in range(SWA_Q_HEADS):
            var = 2 * (head // SWA_GROUP) + head % 2
            cs = slice((head // 2) * LANES, (head // 2 + 1) * LANES)
            s_scr[head] = jnp.where(valid, _dot_nt(q_ref[qrows, cs], kx_scr[var, wrows, :]),
                                    NEG_INF)
        for head in range(SWA_Q_HEADS):
            sink = sink_ref[head]
            sc = s_scr[head]
            m = jnp.maximum(jnp.max(sc, axis=-1, keepdims=True), sink)
            pexp = jnp.exp(sc - m)
            den = jnp.sum(pexp, axis=-1, keepdims=True) + jnp.exp(sink - m)
            p_scr[head] = (pexp * (1.0 / den)).astype(BF16)
        for cc in range(SWA_Q_HEADS // 2):
            g = (2 * cc) // SWA_GROUP
            acc = (_dot(p_scr[2 * cc], vx_scr[2 * g, wrows, :])
                   + _dot(p_scr[2 * cc + 1], vx_scr[2 * g + 1, wrows, :]))
            o_ref[qrows, cc * LANES:(cc + 1) * LANES] = acc.astype(o_ref.dtype)
        return carry

    lax.fori_loop(0, tq // blk, block_body, 0)


def _swa(qkv, sinks, *, tq=TM):
    b, s, _ = qkv.shape
    nq = SWA_Q_HEADS * SWA_HEAD_DIM
    nkv = 2 * SWA_KV_HEADS * SWA_HEAD_DIM
    blk = SWA_WINDOW
    return pl.pallas_call(
        functools.partial(_swa_kernel, tq=tq),
        grid=(b, s // tq),
        in_specs=[
            pl.BlockSpec((None, tq, nq), lambda i, j: (i, j, 0)),
            pl.BlockSpec((None, tq, nkv), lambda i, j: (i, j, nq // nkv)),
            pl.BlockSpec((None, blk, nkv),
                         lambda i, j: (i, jnp.maximum(j * (tq // blk) - 1, 0), nq // nkv)),
            pl.BlockSpec(memory_space=pltpu.SMEM),
        ],
        out_specs=pl.BlockSpec((None, tq, nq), lambda i, j: (i, j, 0)),
        out_shape=jax.ShapeDtypeStruct((b, s, nq), BF16),
        scratch_shapes=[
            pltpu.VMEM((2 * SWA_KV_HEADS, blk + tq, LANES), BF16),
            pltpu.VMEM((2 * SWA_KV_HEADS, blk + tq, LANES), BF16),
            pltpu.VMEM((SWA_Q_HEADS, blk, 2 * blk), F32),
            pltpu.VMEM((SWA_Q_HEADS, blk, 2 * blk), BF16),
        ],
        compiler_params=_params(2),
        name="swa",
    )(qkv, qkv, qkv, sinks)


def _ffn_kernel(te_ref, nused_ref, x_ref, g_ref, w1_ref, w3_ref, w2_ref, o_ref, h_scr, acc_scr,
                *, add_residual, row_tiles, nf):
    i = pl.program_id(0)
    f = pl.program_id(1)
    used = i < nused_ref[0]

    nchunk = o_ref.shape[0] // acc_scr.shape[0] if row_tiles else 1
    tm, d = acc_scr.shape

    @pl.when(used & (f == 0))
    def _():
        if row_tiles:
            xs = [x_ref[pl.ds(s, tm, stride=nchunk), :] for s in range(nchunk)]
            ss = xs[0] * xs[0]
            for x in xs[1:]:
                ss = ss + x * x
            inv = lax.rsqrt(jnp.sum(ss, axis=-1, keepdims=True) * (1.0 / d) + RMS_EPS)
            for s, x in enumerate(xs):
                cs = slice(s * LANES, (s + 1) * LANES)
                h_scr[:, cs] = (x * inv * g_ref[:, cs]).astype(BF16)
        else:
            h_scr[...] = _rms_rows(x_ref[...], g_ref[...]).astype(BF16)
        acc_scr[...] = jnp.zeros_like(acc_scr)

    @pl.when(used)
    def _():
        hb = h_scr[...]
        a = _dot(hb, w1_ref[...].astype(BF16))
        b = _dot(hb, w3_ref[...].astype(BF16))
        acc_scr[...] += _dot((_silu(a) * b).astype(BF16), w2_ref[...].astype(BF16))

    @pl.when(f == nf - 1)
    def _():
        @pl.when(used)
        def _():
            if row_tiles:
                for s in range(nchunk):
                    o_ref[pl.ds(s, tm, stride=nchunk), :] = acc_scr[:, s * LANES:(s + 1) * LANES]
            else:
                y = acc_scr[...]
                o_ref[...] = (x_ref[...] + y) if add_residual else y

        @pl.when(jnp.logical_not(used))
        def _():
            o_ref[...] = jnp.zeros_like(o_ref)


def _ffn(tile_expert, n_used, x, gain, w1, w3, w2, *, add_residual, row_tiles=False,
         tm=TM_FFN, tf=TF):
    d = w1.shape[1]
    r = x.shape[0] * x.shape[1] // d
    nf = w1.shape[2] // tf
    xblk = (tm * d // LANES, LANES) if row_tiles else (tm, d)

    def fsel(i, f, nused):
        return jnp.where(i < nused[0], f, 0)

    grid_spec = pltpu.PrefetchScalarGridSpec(
        num_scalar_prefetch=2,
        grid=(r // tm, nf),
        in_specs=[
            pl.BlockSpec(xblk, lambda i, f, te, nu: (i, 0)),
            pl.BlockSpec((1, d), lambda i, f, te, nu: (0, 0)),
            pl.BlockSpec((None, d, tf), lambda i, f, te, nu: (te[i], 0, fsel(i, f, nu))),
            pl.BlockSpec((None, d, tf), lambda i, f, te, nu: (te[i], 0, fsel(i, f, nu))),
            pl.BlockSpec((None, tf, d), lambda i, f, te, nu: (te[i], fsel(i, f, nu), 0)),
        ],
        out_specs=pl.BlockSpec(xblk, lambda i, f, te, nu: (i, 0)),
        scratch_shapes=[pltpu.VMEM((tm, d), BF16), pltpu.VMEM((tm, d), F32)],
    )
    return pl.pallas_call(
        functools.partial(_ffn_kernel, add_residual=add_residual, row_tiles=row_tiles, nf=nf),
        grid_spec=grid_spec,
        out_shape=jax.ShapeDtypeStruct(x.shape, F32),
        compiler_params=_params(2),
        name="ffn_res" if add_residual else "ffn_experts",
    )(tile_expert, n_used, x, gain.reshape(1, d), w1, w3, w2)


def _router_kernel(x_ref, g_ref, whi_ref, wlo_ref, tri_ref, info_ref, cnt_ref, carry_scr):
    @pl.when(pl.program_id(0) == 0)
    def _():
        carry_scr[...] = jnp.zeros_like(carry_scr)

    h = _rms_rows(x_ref[...], g_ref[...])
    hhi, hlo = _split2(h)
    whi = whi_ref[...]
    logits = _dot(hhi, whi) + _dot(hlo, whi) + _dot(hhi, wlo_ref[...])
    lane = lax.broadcasted_iota(jnp.int32, logits.shape, 1).astype(F32)
    lg = jnp.where(lane < N_EXPERTS, logits, -jnp.inf)
    v1 = jnp.max(lg, axis=-1, keepdims=True)
    i1 = jnp.min(jnp.where(lg == v1, lane, float(LANES)), axis=-1, keepdims=True)
    lg2 = jnp.where(lane == i1, -jnp.inf, lg)
    v2 = jnp.max(lg2, axis=-1, keepdims=True)
    i2 = jnp.min(jnp.where(lg2 == v2, lane, float(LANES)), axis=-1, keepdims=True)
    e = jnp.exp(v2 - v1)
    gate1 = 1.0 / (1.0 + e)
    gate2 = e / (1.0 + e)
    m1 = lane == i1
    m2 = lane == i2
    member = (m1 | m2).astype(F32)
    carry = carry_scr[0:1, :]
    rank = _dot(tri_ref[...], member.astype(BF16)) + carry
    r1 = jnp.sum(jnp.where(m1, rank, 0.0), axis=-1, keepdims=True)
    r2 = jnp.sum(jnp.where(m2, rank, 0.0), axis=-1, keepdims=True)
    total = carry + jnp.sum(member, axis=0, keepdims=True)
    carry_scr[...] = jnp.broadcast_to(total, carry_scr.shape)
    cnt_ref[...] = jnp.broadcast_to(total, cnt_ref.shape)
    info = jnp.where(lane == 0, i1, 0.0)
    info = jnp.where(lane == 1, i2, info)
    info = jnp.where(lane == 2, gate1, info)
    info = jnp.where(lane == 3, gate2, info)
    info = jnp.where(lane == 4, r1, info)
    info = jnp.where(lane == 5, r2, info)
    info_ref[...] = info


def _router(x2d, gain, w_router, *, tm=TM):
    t, d = x2d.shape
    wpad = jnp.zeros((d, LANES), F32).at[:, :N_EXPERTS].set(w_router)
    whi = wpad.astype(BF16)
    wlo = (wpad - whi.astype(F32)).astype(BF16)
    tri = (jnp.arange(tm)[:, None] > jnp.arange(tm)[None, :]).astype(BF16)
    const = lambda shape: pl.BlockSpec(shape, lambda i: (0,) * len(shape))
    return pl.pallas_call(
        _router_kernel,
        grid=(t // tm,),
        in_specs=[pl.BlockSpec((tm, d), lambda i: (i, 0)), const((1, d)), const((d, LANES)),
                  const((d, LANES)), const((tm, tm))],
        out_specs=[pl.BlockSpec((tm, LANES), lambda i: (i, 0)), const((SUBLANES, LANES))],
        out_shape=[jax.ShapeDtypeStruct((t, LANES), F32),
                   jax.ShapeDtypeStruct((SUBLANES, LANES), F32)],
        scratch_shapes=[pltpu.VMEM((SUBLANES, LANES), F32)],
        compiler_params=_params(1),
        name="router",
    )(x2d, gain.reshape(1, d), whi, wlo, tri)


def _for_each_row(n_rows, fn):
    def body(g, c):
        for u in range(COPY_UNROLL):
            fn(g * COPY_UNROLL + u)
        return c
    lax.fori_loop(0, n_rows // COPY_UNROLL, body, 0)


def _tile_of(ref, p):
    return ref.at[pl.ds(pl.multiple_of(p * SUBLANES, SUBLANES), SUBLANES)]


def _dispatch_kernel(pad_start_ref, pad_cnt_ref, pos_ref, x_ref, xs_ref, rows_scr, zero_scr, sem,
                     *, tm):
    nchunk = x_ref.shape[1] // LANES

    @pl.when(pl.program_id(0) == 0)
    def _():
        zero_scr[...] = jnp.zeros_like(zero_scr)

        def pad_copy(e, r):
            return pltpu.make_async_copy(zero_scr.at[pl.ds(0, SUBLANES)],
                                         _tile_of(xs_ref, pad_start_ref[e] + r), sem)

        def tail_copy(j):
            n = zero_scr.shape[0]
            dst = pl.multiple_of((pad_start_ref[N_EXPERTS] + j * tm) * SUBLANES, n)
            return pltpu.make_async_copy(zero_scr, xs_ref.at[pl.ds(dst, n)], sem)

        for e in range(N_EXPERTS):
            lax.fori_loop(0, pad_cnt_ref[e], lambda r, c: (pad_copy(e, r).start(), c)[1], 0)
        lax.fori_loop(0, pad_cnt_ref[N_EXPERTS], lambda j, c: (tail_copy(j).start(), c)[1], 0)
        for e in range(N_EXPERTS):
            lax.fori_loop(0, pad_cnt_ref[e], lambda r, c: (pad_copy(e, r).wait(), c)[1], 0)
        lax.fori_loop(0, pad_cnt_ref[N_EXPERTS], lambda j, c: (tail_copy(j).wait(), c)[1], 0)

    for s in range(nchunk):
        rows_scr[pl.ds(s, tm, stride=nchunk), :] = x_ref[:, s * LANES:(s + 1) * LANES]

    def copy(r, k):
        return pltpu.make_async_copy(_tile_of(rows_scr, r),
                                     _tile_of(xs_ref, pos_ref[0, 0, 2 * r + k]), sem)

    def start(r):
        for k in range(2):
            copy(r, k).start(priority=k)

    def wait(r):
        for k in range(2):
            copy(r, k).wait()

    _for_each_row(tm, start)
    _for_each_row(tm, wait)


def _dispatch(pos, pad_start, pad_cnt, x2d, n_rows, *, tm=TD):
    t, d = x2d.shape
    nchunk = d // LANES
    pos3 = pos.reshape(t // tm, 1, 2 * tm)
    grid_spec = pltpu.PrefetchScalarGridSpec(
        num_scalar_prefetch=2,
        grid=(t // tm,),
        in_specs=[
            pl.BlockSpec((1, 1, 2 * tm), lambda i, ps, pc: (i, 0, 0), memory_space=pltpu.SMEM),
            pl.BlockSpec((tm, d), lambda i, ps, pc: (i, 0)),
        ],
        out_specs=pl.BlockSpec(memory_space=pl.ANY),
        scratch_shapes=[pltpu.VMEM((tm * nchunk, LANES), F32),
                        pltpu.VMEM((tm * nchunk, LANES), F32), pltpu.SemaphoreType.DMA],
    )
    return pl.pallas_call(
        functools.partial(_dispatch_kernel, tm=tm),
        grid_spec=grid_spec,
        out_shape=jax.ShapeDtypeStruct((n_rows * nchunk, LANES), F32),
        compiler_params=_params(1),
        name="dispatch",
    )(pad_start, pad_cnt, pos3, x2d)


def _combine_kernel(pos_ref, x_ref, gate_ref, y_ref, o_ref, buf, sem, *, tm):
    nchunk = x_ref.shape[1] // LANES

    def copy(r, k):
        return pltpu.make_async_copy(_tile_of(y_ref, pos_ref[0, 0, 2 * r + k]),
                                     _tile_of(buf.at[k], r), sem)

    def start(r):
        for k in range(2):
            copy(r, k).start(priority=k)

    def wait(r):
        for k in range(2):
            copy(r, k).wait()

    _for_each_row(tm, start)
    _for_each_row(tm, wait)
    g = gate_ref[...]
    for s in range(nchunk):
        cs = slice(s * LANES, (s + 1) * LANES)
        rows = pl.ds(s, tm, stride=nchunk)
        o_ref[:, cs] = x_ref[:, cs] + g[:, 2:3] * buf[0, rows, :] + g[:, 3:4] * buf[1, rows, :]


def _combine(pos, x2d, info, y, *, tm=TD):
    t, d = x2d.shape
    pos3 = pos.reshape(t // tm, 1, 2 * tm)
    return pl.pallas_call(
        functools.partial(_combine_kernel, tm=tm),
        grid=(t // tm,),
        in_specs=[
            pl.BlockSpec((1, 1, 2 * tm), lambda i: (i, 0, 0), memory_space=pltpu.SMEM),
            pl.BlockSpec((tm, d), lambda i: (i, 0)),
            pl.BlockSpec((tm, LANES), lambda i: (i, 0)),
            pl.BlockSpec(memory_space=pl.ANY),
        ],
        out_specs=pl.BlockSpec((tm, d), lambda i: (i, 0)),
        out_shape=jax.ShapeDtypeStruct((t, d), F32),
        scratch_shapes=[pltpu.VMEM((2, tm * d // LANES, LANES), F32), pltpu.SemaphoreType.DMA],
        compiler_params=_params(1),
        name="combine",
    )(pos3, x2d, info, y)


def _even_layer(x2d, bsz, seqlen, norm1, w_in, conv_qkv, a_log, dt_bias, o_norm, conv_sc, w_out,
                norm2, w1, w3, w2):
    w4 = DN_WIDTH
    c_ab = 4 * w4
    c_sc = c_ab + 2 * DN_HEADS
    w_main = jnp.concatenate([w_in[:, :c_ab], w_in[:, c_sc:]], axis=1).astype(BF16)
    w_ab = jnp.zeros((D_MODEL, LANES), F32).at[:, :2 * DN_HEADS].set(w_in[:, c_ab:c_sc])
    proj, ab = _inproj_even(x2d, norm1, w_main, w_ab, conv_qkv, conv_sc, seqlen)
    o = _deltanet(proj.reshape(bsz, seqlen, -1), ab.reshape(bsz, seqlen, LANES),
                  a_log, dt_bias, o_norm)
    x2d = _outproj([(o.reshape(bsz * seqlen, w4), w4, 0), (proj, SC_WIDTH, 4)],
                   w_out.astype(BF16), x2d)
    n_tiles = x2d.shape[0] // TM_FFN
    te = jnp.zeros((n_tiles,), jnp.int32)
    nused = jnp.full((1,), n_tiles, jnp.int32)
    return _ffn(te, nused, x2d, norm2, w1[None], w3[None], w2[None], add_residual=True)


def _odd_layer(x2d, bsz, seqlen, norm1, w_in, q_norm, k_norm, sinks, w_out, norm2, w_router,
               w1, w3, w2, *, tm=TM_FFN):
    t = x2d.shape[0]
    nq = SWA_Q_HEADS * SWA_HEAD_DIM
    qkv = _inproj_odd(x2d, norm1, w_in.astype(BF16), q_norm, k_norm)
    att = _swa(qkv.reshape(bsz, seqlen, -1), sinks)
    x2d = _outproj([(att.reshape(t, nq), nq, 0)], w_out.astype(BF16), x2d)

    info, cnt = _router(x2d, norm2, w_router)
    counts = cnt[0, :N_EXPERTS].astype(jnp.int32)
    padded = ((counts + tm - 1) // tm) * tm
    ends = jnp.cumsum(padded)
    offsets = ends - padded
    n_rows = 2 * t + N_EXPERTS * tm
    n_tiles = n_rows // tm
    tile_start = jnp.arange(n_tiles, dtype=jnp.int32) * tm
    tile_expert = jnp.minimum(jnp.sum(tile_start[:, None] >= ends[None, :], axis=1),
                              N_EXPERTS - 1).astype(jnp.int32)
    n_used = (ends[-1:] // tm).astype(jnp.int32)
    idx = info[:, 0:2].astype(jnp.int32)
    pos = (offsets[idx] + info[:, 4:6].astype(jnp.int32)).reshape(-1)

    pad_start = jnp.concatenate([offsets + counts, ends[-1:]]).astype(jnp.int32)
    pad_cnt = jnp.concatenate([padded - counts, (n_rows - ends[-1:]) // TD]).astype(jnp.int32)
    xs = _dispatch(pos, pad_start, pad_cnt, x2d, n_rows)
    ys = _ffn(tile_expert, n_used, xs, norm2, w1, w3, w2, add_residual=False, row_tiles=True,
              tm=tm)
    return _combine(pos, x2d, info, ys)


def kernel(x, ev_norm1, ev_w_in, ev_conv_qkv, ev_a_log, ev_dt_bias, ev_o_norm, ev_conv_sc,
           ev_w_out, ev_norm2, ev_ffn_w1, ev_ffn_w3, ev_ffn_w2, od_norm1, od_w_in, od_q_norm,
           od_k_norm, od_sinks, od_w_out, od_norm2, od_router, od_moe_w1, od_moe_w3, od_moe_w2):
    bsz, seqlen, d = x.shape
    x2d = x.reshape(bsz * seqlen, d)
    depth = ev_norm1.shape[0] + od_norm1.shape[0]
    for layer in range(depth):
        i = layer // 2
        if layer % 2 == 0:
            x2d = _even_layer(x2d, bsz, seqlen, ev_norm1[i], ev_w_in[i], ev_conv_qkv[i],
                              ev_a_log[i], ev_dt_bias[i], ev_o_norm[i], ev_conv_sc[i],
                              ev_w_out[i], ev_norm2[i], ev_ffn_w1[i], ev_ffn_w3[i], ev_ffn_w2[i])
        else:
            x2d = _odd_layer(x2d, bsz, seqlen, od_norm1[i], od_w_in[i], od_q_norm[i],
                             od_k_norm[i], od_sinks[i], od_w_out[i], od_norm2[i], od_router[i],
                             od_moe_w1[i], od_moe_w3[i], od_moe_w2[i])
    return x2d.reshape(bsz, seqlen, d)
```

```python
import functools

import jax
import jax.numpy as jnp
from jax import lax
from jax.experimental import pallas as pl
from jax.experimental.pallas import tpu as pltpu

F32 = jnp.float32
BF16 = jnp.bfloat16

D_MODEL = 1024
RMS_EPS = 1e-6
L2_EPS = 1e-6
NEG_INF = -1e30

DN_HEADS = 4
DN_HEAD_DIM = 128
DN_WIDTH = DN_HEADS * DN_HEAD_DIM
DN_CONV = 4
DN_CHUNK = 64
SC_WIDTH = 512
SC_CONV = 3

SWA_Q_HEADS = 16
SWA_KV_HEADS = 2
SWA_HEAD_DIM = 64
SWA_WINDOW = 128
SWA_GROUP = SWA_Q_HEADS // SWA_KV_HEADS

N_EXPERTS = 8

LANES = 128
SUBLANES = 8
PAIR = 2 * DN_CHUNK

TM = 512
TM_FFN = 1024
TF = 512
TQ = 1024
TD = 512
COPY_UNROLL = 8
VMEM_LIMIT = 56 * 1024 * 1024


def _dot(a, b):
    return jnp.dot(a, b, preferred_element_type=F32)


def _dot_nt(a, b):
    return lax.dot_general(a, b, (((1,), (1,)), ((), ())), preferred_element_type=F32)


def _split2(x):
    hi = x.astype(BF16)
    lo = (x - hi.astype(F32)).astype(BF16)
    return hi, lo


def _split3(x):
    hi = x.astype(BF16)
    r = x - hi.astype(F32)
    mid = r.astype(BF16)
    lo = (r - mid.astype(F32)).astype(BF16)
    return hi, mid, lo


def _dot_mask_l(mask_bf16, x):
    h, m, l = _split3(x)
    return _dot(mask_bf16, h) + _dot(mask_bf16, m) + _dot(mask_bf16, l)


def _silu(x):
    return x * (1.0 / (1.0 + jnp.exp(-x)))


def _rms_rows(x, gain):
    ms = jnp.mean(x * x, axis=-1, keepdims=True)
    return x * lax.rsqrt(ms + RMS_EPS) * gain


def _params(n_axes):
    return pltpu.CompilerParams(dimension_semantics=("arbitrary",) * n_axes,
                                vmem_limit_bytes=VMEM_LIMIT)


def _inproj_even_kernel(x_ref, g_ref, w_ref, wab_hi_ref, wab_lo_ref, cq_ref, cs_ref,
                        o_ref, oab_ref, xe_scr, *, tm, tiles_per_seq):
    w4 = DN_WIDTH
    ncv = 3 * w4 + SC_WIDTH
    h = _rms_rows(x_ref[...], g_ref[...])
    hb = h.astype(BF16)

    def proj(part):
        return _dot(hb, w_ref[:, part * w4:(part + 1) * w4])

    @pl.when(pl.program_id(0) == 0)
    def _():
        xe_scr[tm:tm + SUBLANES, :] = jnp.zeros((SUBLANES, ncv), F32)

    live = pl.program_id(0) % tiles_per_seq > 0
    xe_scr[0:SUBLANES, :] = jnp.where(live, xe_scr[tm:tm + SUBLANES, :], 0.0)
    def put(part):
        xe_scr[SUBLANES:SUBLANES + tm, part * w4:(part + 1) * w4] = proj(part)

    def conv(cols, wc_ref, width):
        xe = xe_scr[:, cols]
        acc = None
        for j in range(width):
            back = width - 1 - j
            xs = xe if back == 0 else pltpu.roll(xe, back, axis=0)
            term = xs[SUBLANES:SUBLANES + tm] * wc_ref[j:j + 1, :]
            acc = term if acc is None else acc + term
        return acc

    def mixer_input(part):
        cols = slice(part * w4, (part + 1) * w4)
        y = _silu(conv(cols, cq_ref.at[:, cols], DN_CONV))
        if part == 2:
            o_ref[:, cols] = y.astype(o_ref.dtype)
            return
        for hh in range(DN_HEADS):
            hc = slice(hh * DN_HEAD_DIM, (hh + 1) * DN_HEAD_DIM)
            yh = y[:, hc]
            inv = lax.rsqrt(jnp.sum(yh * yh, axis=-1, keepdims=True) + L2_EPS)
            if part == 0:
                inv = inv * (DN_HEAD_DIM ** -0.5)
            o_ref[:, part * w4 + hh * DN_HEAD_DIM:part * w4 + (hh + 1) * DN_HEAD_DIM] = (
                yh * inv).astype(o_ref.dtype)

    put(0)
    put(1)
    mixer_input(0)
    put(2)
    mixer_input(1)
    xe_scr[SUBLANES:SUBLANES + tm, 3 * w4:ncv] = proj(5) * proj(6)
    mixer_input(2)
    bg = proj(4)
    ysc = bg * conv(slice(3 * w4, ncv), cs_ref, SC_CONV)
    o_ref[:, 4 * w4:5 * w4] = ysc.astype(o_ref.dtype)
    o_ref[:, 3 * w4:4 * w4] = _silu(proj(3)).astype(o_ref.dtype)

    hlo = (h - hb.astype(F32)).astype(BF16)
    whi = wab_hi_ref[...]
    oab_ref[...] = _dot(hb, whi) + _dot(hlo, whi) + _dot(hb, wab_lo_ref[...])


def _inproj_even(x2d, gain, w_bf16, wab, conv_qkv, conv_sc, seqlen, *, tm=TM):
    t, d = x2d.shape
    n = w_bf16.shape[1]
    w4 = DN_WIDTH
    wab_hi = wab.astype(BF16)
    wab_lo = (wab - wab_hi.astype(F32)).astype(BF16)
    const = lambda shape: pl.BlockSpec(shape, lambda i: (0,) * len(shape))
    return pl.pallas_call(
        functools.partial(_inproj_even_kernel, tm=tm, tiles_per_seq=seqlen // tm),
        grid=(t // tm,),
        in_specs=[
            pl.BlockSpec((tm, d), lambda i: (i, 0)),
            const((1, d)), const((d, n)), const((d, LANES)), const((d, LANES)),
            const((DN_CONV, 3 * w4)), const((SC_CONV, SC_WIDTH)),
        ],
        out_specs=[pl.BlockSpec((tm, 5 * w4), lambda i: (i, 0)),
                   pl.BlockSpec((tm, LANES), lambda i: (i, 0))],
        out_shape=[jax.ShapeDtypeStruct((t, 5 * w4), BF16),
                   jax.ShapeDtypeStruct((t, LANES), F32)],
        scratch_shapes=[pltpu.VMEM((tm + SUBLANES, 3 * w4 + SC_WIDTH), F32)],
        compiler_params=_params(1),
        name="inproj_even",
    )(x2d, gain.reshape(1, d), w_bf16, wab_hi, wab_lo, conv_qkv, conv_sc)


def _inproj_odd_kernel(x_ref, g_ref, w_ref, qw_ref, kw_ref, bd_ref, o_ref, y_scr, ms_scr):
    nq = SWA_Q_HEADS * SWA_HEAD_DIM
    n_norm = nq // LANES + 1
    hb = _rms_rows(x_ref[...], g_ref[...]).astype(BF16)
    bd = bd_ref[...]
    for c in range(n_norm + 1):
        cs = slice(c * LANES, (c + 1) * LANES)
        y_scr[:, cs] = _dot(hb, w_ref[:, cs])
    for c in range(n_norm):
        cs = slice(c * LANES, (c + 1) * LANES)
        y = y_scr[:, cs]
        hi, lo = _split2(y * y)
        ms_scr[:, cs] = _dot(hi, bd) + _dot(lo, bd)
    for c in range(n_norm):
        cs = slice(c * LANES, (c + 1) * LANES)
        inv = lax.rsqrt(ms_scr[:, cs] * (1.0 / SWA_HEAD_DIM) + RMS_EPS)
        if c < n_norm - 1:
            y = y_scr[:, cs] * inv * qw_ref[...] * (SWA_HEAD_DIM ** -0.5)
        else:
            y = y_scr[:, cs] * inv * kw_ref[...]
        o_ref[:, cs] = y.astype(o_ref.dtype)
    vs = slice(n_norm * LANES, (n_norm + 1) * LANES)
    o_ref[:, vs] = y_scr[:, vs].astype(o_ref.dtype)


def _inproj_odd(x2d, gain, w_bf16, q_norm, k_norm, *, tm=TM):
    t, d = x2d.shape
    n = w_bf16.shape[1]
    rep = LANES // SWA_HEAD_DIM
    ids = jnp.arange(LANES) // SWA_HEAD_DIM
    bd = (ids[:, None] == ids[None, :]).astype(BF16)
    const = lambda shape: pl.BlockSpec(shape, lambda i: (0,) * len(shape))
    return pl.pallas_call(
        _inproj_odd_kernel,
        grid=(t // tm,),
        in_specs=[pl.BlockSpec((tm, d), lambda i: (i, 0)), const((1, d)), const((d, n)),
                  const((1, LANES)), const((1, LANES)), const((LANES, LANES))],
        out_specs=pl.BlockSpec((tm, n), lambda i: (i, 0)),
        out_shape=jax.ShapeDtypeStruct((t, n), BF16),
        scratch_shapes=[pltpu.VMEM((tm, n), F32), pltpu.VMEM((tm, n - LANES), F32)],
        compiler_params=_params(1),
        name="inproj_odd",
    )(x2d, gain.reshape(1, d), w_bf16, jnp.tile(q_norm, rep).reshape(1, LANES),
      jnp.tile(k_norm, rep).reshape(1, LANES), bd)


def _deltanet_kernel(q_ref, k_ref, v_ref, sz_ref, ysc_ref, ab_ref, gvec_ref, onorm_ref, lvlm_ref,
                     res_ref, wout_ref, o_ref,
                     gb_scr, oh_scr, s_scr, tinv_scr, lm_scr, x_scr, at_scr, cat_scr, kdt_scr,
                     qd_scr, ge_scr, uw_scr, mm_scr, bb_scr, qp_scr, op_scr, *, ts):
    n_pairs = ts // PAIR
    n_prob = n_pairs * DN_HEADS

    @pl.when(pl.program_id(1) == 0)
    def _():
        s_scr[...] = jnp.zeros_like(s_scr)

    ab = ab_ref[...]
    lane_t = lax.broadcasted_iota(jnp.int32, ab.shape, 1)
    z = ab + gvec_ref[0:1, :]
    softplus = jnp.maximum(z, 0.0) + jnp.log1p(jnp.exp(-jnp.abs(z)))
    g = -jnp.exp(gvec_ref[1:2, :]) * softplus
    gb_scr[...] = jnp.where(lane_t < DN_HEADS, g, 1.0 / (1.0 + jnp.exp(-ab)))

    lane = lax.broadcasted_iota(jnp.int32, (PAIR, LANES), 1)
    row = lax.broadcasted_iota(jnp.int32, (PAIR, LANES), 0)
    same = (row // DN_CHUNK) == (lane // DN_CHUNK)
    incl = same & (row >= lane)
    strict = same & (row > lane)
    cs_mask = incl.astype(BF16)
    eye = (row == lane).astype(F32)
    lvl1 = ((row // 2) == (lane // 2)) & (row > lane)
    first_chunk = row < DN_CHUNK
    first_chunk_col = lane < DN_CHUNK

    def stage1(p):
        rows = slice(p * PAIR, (p + 1) * PAIR)
        gbp = gb_scr[rows, :]
        gcc = _dot_mask_l(cs_mask, gbp)
        for h in range(DN_HEADS):
            i = p * DN_HEADS + h
            hc = slice(h * DN_HEAD_DIM, (h + 1) * DN_HEAD_DIM)
            gc = jnp.broadcast_to(gcc[:, h:h + 1], (PAIR, LANES))
            beta = jnp.broadcast_to(gbp[:, DN_HEADS + h:DN_HEADS + h + 1], (PAIR, LANES))
            dec = jnp.exp(jnp.where(incl, gc - gc.T, NEG_INF))
            egc = jnp.exp(gc)
            gl0 = gc[DN_CHUNK - 1:DN_CHUNK, :]
            gl1 = gc[PAIR - 1:PAIR, :]
            ekd = jnp.exp(jnp.where(first_chunk, gl0, gl1) - gc)
            ge_scr[i, 0:1, :] = jnp.exp(gl0)
            ge_scr[i, 1:2, :] = jnp.exp(gl1)
            qn = q_ref[rows, hc].astype(F32)
            kn_b = k_ref[rows, hc]
            kn = kn_b.astype(F32)
            kb = kn * beta
            both = _dot_nt(jnp.concatenate([kb.astype(BF16), q_ref[rows, hc]], axis=0), kn_b)
            lmat = jnp.where(strict, both[0:PAIR] * dec, 0.0)
            tinv_scr[i] = eye - jnp.where(lvl1, lmat, 0.0)
            lm_scr[i] = lmat.astype(BF16)
            at_scr[i] = (both[PAIR:2 * PAIR] * dec).astype(BF16)
            cat_scr[i, :, 0:LANES] = (kb * egc).astype(BF16)
            cat_scr[i, :, LANES:2 * LANES] = (v_ref[rows, hc].astype(F32) * beta).astype(BF16)
            qd_scr[i] = qn * egc
            kdt = (kn * ekd).T
            kdt_scr[i, 0] = jnp.where(first_chunk_col, kdt, 0.0).astype(BF16)
            kdt_scr[i, 1] = jnp.where(first_chunk_col, 0.0, kdt).astype(BF16)

    def stage2_left(lvl, probs):
        for i in probs:
            x = _dot(tinv_scr[i].astype(BF16), lm_scr[i] * lvlm_ref[lvl])
            x_scr[i] = x.astype(BF16)

    def stage2_right(lvl, probs):
        for i in probs:
            t = tinv_scr[i]
            tinv_scr[i] = t - _dot(x_scr[i], t.astype(BF16))

    def stage3_uw(probs):
        for i in probs:
            uw_scr[i] = _dot(tinv_scr[i].astype(BF16), cat_scr[i]).astype(BF16)

    def stage3_ops(i):
        uw = uw_scr[i]
        for c in range(2):
            r = _dot(kdt_scr[i, c], uw)
            mm_scr[i, c] = r[:, 0:LANES].astype(BF16)
            bb_scr[i, c] = r[:, LANES:2 * LANES]
        r = _dot(at_scr[i], uw)
        qp_scr[i] = (qd_scr[i] - r[:, 0:LANES]).astype(BF16)
        op_scr[i] = r[:, LANES:2 * LANES]

    state = [s_scr[h] for h in range(DN_HEADS)]

    def stage4(p):
        for h in range(DN_HEADS):
            i = p * DN_HEADS + h
            hc = slice(h * DN_HEAD_DIM, (h + 1) * DN_HEAD_DIM)
            st = state[h]
            for c in range(2):
                cr = slice(c * DN_CHUNK, (c + 1) * DN_CHUNK)
                sb = st.astype(BF16)
                oh_scr[p * PAIR + c * DN_CHUNK:p * PAIR + (c + 1) * DN_CHUNK, hc] = (
                    _dot(qp_scr[i, cr, :], sb) + op_scr[i, cr, :])
                st = st * ge_scr[i, c:c + 1, :] - _dot(mm_scr[i, c], sb) + bb_scr[i, c]
            state[h] = st

    probs = range(n_prob)
    for p in range(n_pairs):
        stage1(p)
    for lvl in range(lvlm_ref.shape[0]):
        stage2_left(lvl, probs)
        stage2_right(lvl, probs)
    stage3_uw(probs)
    for i in probs:
        stage3_ops(i)
    def stage5(p):
        rows = slice(p * PAIR, (p + 1) * PAIR)
        gated = []
        for h in range(DN_HEADS):
            hc = slice(h * DN_HEAD_DIM, (h + 1) * DN_HEAD_DIM)
            o = _rms_rows(oh_scr[rows, hc], onorm_ref[...])
            gated.append((o * sz_ref[rows, hc].astype(F32)).astype(BF16))
        mix_o = jnp.concatenate(gated, axis=1)
        o_ref[rows, :] = (res_ref[rows, :] + _dot(mix_o, wout_ref[0:DN_WIDTH, :])
                          + _dot(ysc_ref[rows, :], wout_ref[DN_WIDTH:DN_WIDTH + SC_WIDTH, :]))

    for p in range(n_pairs):
        stage4(p)
        if p > 0:
            stage5(p - 1)
    stage5(n_pairs - 1)
    for h in range(DN_HEADS):
        s_scr[h] = state[h]


def _deltanet(proj, ab, a_log, dt_bias, o_norm, res, w_out, *, ts=TM):
    b, s, _ = proj.shape
    d = res.shape[2]
    w4 = DN_WIDTH
    n_prob = (ts // PAIR) * DN_HEADS
    gvec = jnp.zeros((SUBLANES, LANES), F32)
    gvec = gvec.at[0, :DN_HEADS].set(dt_bias).at[1, :DN_HEADS].set(a_log)
    r = jnp.arange(PAIR)[:, None]
    c = jnp.arange(LANES)[None, :]
    lvl_masks = []
    blk = 2
    while blk < DN_CHUNK:
        lvl_masks.append((r // (2 * blk) == c // (2 * blk)) & (r // blk != c // blk) & (r > c))
        blk *= 2
    lvlm = jnp.stack(lvl_masks).astype(BF16)
    nl = lvlm.shape[0]
    blk_f32 = pltpu.VMEM((n_prob, PAIR, LANES), F32)
    blk_b16 = pltpu.VMEM((n_prob, PAIR, LANES), BF16)
    col = lambda cb: pl.BlockSpec((None, ts, w4), functools.partial(
        lambda i, j, cb: (i, j, cb), cb=cb))
    return pl.pallas_call(
        functools.partial(_deltanet_kernel, ts=ts),
        grid=(b, s // ts),
        in_specs=[
            col(0), col(1), col(2), col(3), col(4),
            pl.BlockSpec((None, ts, LANES), lambda i, j: (i, j, 0)),
            pl.BlockSpec((SUBLANES, LANES), lambda i, j: (0, 0)),
            pl.BlockSpec((1, DN_HEAD_DIM), lambda i, j: (0, 0)),
            pl.BlockSpec((nl, PAIR, LANES), lambda i, j: (0, 0, 0)),
            pl.BlockSpec((None, ts, d), lambda i, j: (i, j, 0)),
            pl.BlockSpec(w_out.shape, lambda i, j: (0, 0)),
        ],
        out_specs=pl.BlockSpec((None, ts, d), lambda i, j: (i, j, 0)),
        out_shape=jax.ShapeDtypeStruct((b, s, d), F32),
        scratch_shapes=[
            pltpu.VMEM((ts, LANES), F32),
            pltpu.VMEM((ts, w4), F32),
            pltpu.VMEM((DN_HEADS, DN_HEAD_DIM, DN_HEAD_DIM), F32),
            blk_f32, blk_b16, blk_b16, blk_b16,
            pltpu.VMEM((n_prob, PAIR, 2 * LANES), BF16),
            pltpu.VMEM((n_prob, 2, DN_HEAD_DIM, PAIR), BF16),
            blk_f32,
            pltpu.VMEM((n_prob, SUBLANES, LANES), F32),
            pltpu.VMEM((n_prob, PAIR, 2 * LANES), BF16),
            pltpu.VMEM((n_prob, 2, DN_HEAD_DIM, DN_HEAD_DIM), BF16),
            pltpu.VMEM((n_prob, 2, DN_HEAD_DIM, DN_HEAD_DIM), F32),
            blk_b16, blk_f32,
        ],
        compiler_params=_params(2),
        name="deltanet",
    )(proj, proj, proj, proj, proj, ab, gvec, o_norm.reshape(1, -1), lvlm, res, w_out)


def _swa_kernel(q_ref, kv_ref, halo_ref, sink_ref, res_ref, wout_ref, o_ref,
                kx_scr, vx_scr, s_scr, p_scr, *, tq):
    s_idx = pl.program_id(1)
    blk = SWA_WINDOW
    hd = SWA_HEAD_DIM

    lane = lax.broadcasted_iota(jnp.int32, (blk + tq, LANES), 1)
    lo_half = lane < hd
    live = (s_idx > 0).astype(F32)
    for src, scr in ((0, kx_scr), (1, vx_scr)):
        cols = slice(src * LANES, (src + 1) * LANES)
        win = jnp.concatenate([halo_ref[:, cols].astype(F32) * live, kv_ref[:, cols].astype(F32)],
                              axis=0)
        swp = pltpu.roll(win, hd, axis=1)
        scr[0] = jnp.where(lo_half, win, 0.0).astype(BF16)
        scr[1] = jnp.where(lo_half, 0.0, swp).astype(BF16)
        scr[2] = jnp.where(lo_half, swp, 0.0).astype(BF16)
        scr[3] = jnp.where(lo_half, 0.0, win).astype(BF16)

    qi = lax.broadcasted_iota(jnp.int32, (blk, 2 * blk), 0)
    kj = lax.broadcasted_iota(jnp.int32, (blk, 2 * blk), 1)
    band = (kj > qi) & (kj <= qi + blk)

    def block_body(j, carry):
        r0 = pl.multiple_of(j * blk, blk)
        qrows = pl.ds(r0, blk)
        wrows = pl.ds(r0, 2 * blk)
        valid = band & ((kj >= blk) | (s_idx > 0) | (j > 0))
        for head in range(SWA_Q_HEADS):
            var = 2 * (head // SWA_GROUP) + head % 2
            cs = slice((head // 2) * LANES, (head // 2 + 1) * LANES)
            s_scr[head] = jnp.where(valid, _dot_nt(q_ref[qrows, cs], kx_scr[var, wrows, :]),
                                    NEG_INF)
        for head in range(SWA_Q_HEADS):
            sink = sink_ref[head]
            sc = s_scr[head]
            m = jnp.maximum(jnp.max(sc, axis=-1, keepdims=True), sink)
            pexp = jnp.exp(sc - m)
            den = jnp.sum(pexp, axis=-1, keepdims=True) + jnp.exp(sink - m)
            p_scr[head] = (pexp * (1.0 / den)).astype(BF16)
        att = []
        for cc in range(SWA_Q_HEADS // 2):
            g = (2 * cc) // SWA_GROUP
            acc = (_dot(p_scr[2 * cc], vx_scr[2 * g, wrows, :])
                   + _dot(p_scr[2 * cc + 1], vx_scr[2 * g + 1, wrows, :]))
            att.append(acc.astype(BF16))
        o_ref[qrows, :] = res_ref[qrows, :] + _dot(jnp.concatenate(att, axis=1), wout_ref[...])
        return carry

    lax.fori_loop(0, tq // blk, block_body, 0)


def _swa(qkv, sinks, res, w_out, *, tq=TQ):
    b, s, _ = qkv.shape
    d = res.shape[2]
    nq = SWA_Q_HEADS * SWA_HEAD_DIM
    nkv = 2 * SWA_KV_HEADS * SWA_HEAD_DIM
    blk = SWA_WINDOW
    return pl.pallas_call(
        functools.partial(_swa_kernel, tq=tq),
        grid=(b, s // tq),
        in_specs=[
            pl.BlockSpec((None, tq, nq), lambda i, j: (i, j, 0)),
            pl.BlockSpec((None, tq, nkv), lambda i, j: (i, j, nq // nkv)),
            pl.BlockSpec((None, blk, nkv),
                         lambda i, j: (i, jnp.maximum(j * (tq // blk) - 1, 0), nq // nkv)),
            pl.BlockSpec(memory_space=pltpu.SMEM),
            pl.BlockSpec((None, tq, d), lambda i, j: (i, j, 0)),
            pl.BlockSpec(w_out.shape, lambda i, j: (0, 0)),
        ],
        out_specs=pl.BlockSpec((None, tq, d), lambda i, j: (i, j, 0)),
        out_shape=jax.ShapeDtypeStruct((b, s, d), F32),
        scratch_shapes=[
            pltpu.VMEM((2 * SWA_KV_HEADS, blk + tq, LANES), BF16),
            pltpu.VMEM((2 * SWA_KV_HEADS, blk + tq, LANES), BF16),
            pltpu.VMEM((SWA_Q_HEADS, blk, 2 * blk), F32),
            pltpu.VMEM((SWA_Q_HEADS, blk, 2 * blk), BF16),
        ],
        compiler_params=_params(2),
        name="swa",
    )(qkv, qkv, qkv, sinks, res, w_out)


def _ffn_kernel(te_ref, nused_ref, x_ref, g_ref, w1_ref, w3_ref, w2_ref, o_ref, h_scr, acc_scr,
                *, add_residual, row_tiles, nf):
    i = pl.program_id(0)
    f = pl.program_id(1)
    used = i < nused_ref[0]

    nchunk = o_ref.shape[0] // acc_scr.shape[0] if row_tiles else 1
    tm, d = acc_scr.shape

    @pl.when(used & (f == 0))
    def _():
        if row_tiles:
            xs = [x_ref[pl.ds(s, tm, stride=nchunk), :] for s in range(nchunk)]
            ss = xs[0] * xs[0]
            for x in xs[1:]:
                ss = ss + x * x
            inv = lax.rsqrt(jnp.sum(ss, axis=-1, keepdims=True) * (1.0 / d) + RMS_EPS)
            for s, x in enumerate(xs):
                cs = slice(s * LANES, (s + 1) * LANES)
                h_scr[:, cs] = (x * inv * g_ref[:, cs]).astype(BF16)
        else:
            h_scr[...] = _rms_rows(x_ref[...], g_ref[...]).astype(BF16)
        acc_scr[...] = jnp.zeros_like(acc_scr)

    @pl.when(used)
    def _():
        hb = h_scr[...]
        a = _dot(hb, w1_ref[...].astype(BF16))
        b = _dot(hb, w3_ref[...].astype(BF16))
        acc_scr[...] += _dot((_silu(a) * b).astype(BF16), w2_ref[...].astype(BF16))

    @pl.when(f == nf - 1)
    def _():
        @pl.when(used)
        def _():
            if row_tiles:
                for s in range(nchunk):
                    o_ref[pl.ds(s, tm, stride=nchunk), :] = acc_scr[:, s * LANES:(s + 1) * LANES]
            else:
                y = acc_scr[...]
                o_ref[...] = (x_ref[...] + y) if add_residual else y

        @pl.when(jnp.logical_not(used))
        def _():
            o_ref[...] = jnp.zeros_like(o_ref)


def _ffn(tile_expert, n_used, x, gain, w1, w3, w2, *, add_residual, row_tiles=False,
         tm=TM_FFN, tf=TF):
    d = w1.shape[1]
    r = x.shape[0] * x.shape[1] // d
    nf = w1.shape[2] // tf
    xblk = (tm * d // LANES, LANES) if row_tiles else (tm, d)

    def fsel(i, f, nused):
        return jnp.where(i < nused[0], f, 0)

    grid_spec = pltpu.PrefetchScalarGridSpec(
        num_scalar_prefetch=2,
        grid=(r // tm, nf),
        in_specs=[
            pl.BlockSpec(xblk, lambda i, f, te, nu: (i, 0)),
            pl.BlockSpec((1, d), lambda i, f, te, nu: (0, 0)),
            pl.BlockSpec((None, d, tf), lambda i, f, te, nu: (te[i], 0, fsel(i, f, nu))),
            pl.BlockSpec((None, d, tf), lambda i, f, te, nu: (te[i], 0, fsel(i, f, nu))),
            pl.BlockSpec((None, tf, d), lambda i, f, te, nu: (te[i], fsel(i, f, nu), 0)),
        ],
        out_specs=pl.BlockSpec(xblk, lambda i, f, te, nu: (i, 0)),
        scratch_shapes=[pltpu.VMEM((tm, d), BF16), pltpu.VMEM((tm, d), F32)],
    )
    return pl.pallas_call(
        functools.partial(_ffn_kernel, add_residual=add_residual, row_tiles=row_tiles, nf=nf),
        grid_spec=grid_spec,
        out_shape=jax.ShapeDtypeStruct(x.shape, F32),
        compiler_params=_params(2),
        name="ffn_res" if add_residual else "ffn_experts",
    )(tile_expert, n_used, x, gain.reshape(1, d), w1, w3, w2)


def _router_kernel(x_ref, g_ref, whi_ref, wlo_ref, tri_ref, info_ref, cnt_ref, carry_scr):
    @pl.when(pl.program_id(0) == 0)
    def _():
        carry_scr[...] = jnp.zeros_like(carry_scr)

    h = _rms_rows(x_ref[...], g_ref[...])
    hhi, hlo = _split2(h)
    whi = whi_ref[...]
    logits = _dot(hhi, whi) + _dot(hlo, whi) + _dot(hhi, wlo_ref[...])
    lane = lax.broadcasted_iota(jnp.int32, logits.shape, 1).astype(F32)
    lg = jnp.where(lane < N_EXPERTS, logits, -jnp.inf)
    v1 = jnp.max(lg, axis=-1, keepdims=True)
    i1 = jnp.min(jnp.where(lg == v1, lane, float(LANES)), axis=-1, keepdims=True)
    lg2 = jnp.where(lane == i1, -jnp.inf, lg)
    v2 = jnp.max(lg2, axis=-1, keepdims=True)
    i2 = jnp.min(jnp.where(lg2 == v2, lane, float(LANES)), axis=-1, keepdims=True)
    e = jnp.exp(v2 - v1)
    gate1 = 1.0 / (1.0 + e)
    gate2 = e / (1.0 + e)
    m1 = lane == i1
    m2 = lane == i2
    member = (m1 | m2).astype(F32)
    carry = carry_scr[0:1, :]
    rank = _dot(tri_ref[...], member.astype(BF16)) + carry
    r1 = jnp.sum(jnp.where(m1, rank, 0.0), axis=-1, keepdims=True)
    r2 = jnp.sum(jnp.where(m2, rank, 0.0), axis=-1, keepdims=True)
    total = carry + jnp.sum(member, axis=0, keepdims=True)
    carry_scr[...] = jnp.broadcast_to(total, carry_scr.shape)
    cnt_ref[...] = jnp.broadcast_to(total, cnt_ref.shape)
    info = jnp.where(lane == 0, i1, 0.0)
    info = jnp.where(lane == 1, i2, info)
    info = jnp.where(lane == 2, gate1, info)
    info = jnp.where(lane == 3, gate2, info)
    info = jnp.where(lane == 4, r1, info)
    info = jnp.where(lane == 5, r2, info)
    info_ref[...] = info


def _router(x2d, gain, w_router, *, tm=TM):
    t, d = x2d.shape
    wpad = jnp.zeros((d, LANES), F32).at[:, :N_EXPERTS].set(w_router)
    whi = wpad.astype(BF16)
    wlo = (wpad - whi.astype(F32)).astype(BF16)
    tri = (jnp.arange(tm)[:, None] > jnp.arange(tm)[None, :]).astype(BF16)
    const = lambda shape: pl.BlockSpec(shape, lambda i: (0,) * len(shape))
    return pl.pallas_call(
        _router_kernel,
        grid=(t // tm,),
        in_specs=[pl.BlockSpec((tm, d), lambda i: (i, 0)), const((1, d)), const((d, LANES)),
                  const((d, LANES)), const((tm, tm))],
        out_specs=[pl.BlockSpec((tm, LANES), lambda i: (i, 0)), const((SUBLANES, LANES))],
        out_shape=[jax.ShapeDtypeStruct((t, LANES), F32),
                   jax.ShapeDtypeStruct((SUBLANES, LANES), F32)],
        scratch_shapes=[pltpu.VMEM((SUBLANES, LANES), F32)],
        compiler_params=_params(1),
        name="router",
    )(x2d, gain.reshape(1, d), whi, wlo, tri)


def _for_each_row(n_rows, fn):
    def body(g, c):
        for u in range(COPY_UNROLL):
            fn(g * COPY_UNROLL + u)
        return c
    lax.fori_loop(0, n_rows // COPY_UNROLL, body, 0)


def _tile_of(ref, p):
    return ref.at[pl.ds(pl.multiple_of(p * SUBLANES, SUBLANES), SUBLANES)]


def _dispatch_kernel(pad_start_ref, pad_cnt_ref, pos_ref, x_ref, xs_ref, rows_scr, zero_scr, sem,
                     *, tm):
    nchunk = x_ref.shape[1] // LANES

    @pl.when(pl.program_id(0) == 0)
    def _():
        zero_scr[...] = jnp.zeros_like(zero_scr)

        def pad_copy(e, r):
            return pltpu.make_async_copy(zero_scr.at[pl.ds(0, SUBLANES)],
                                         _tile_of(xs_ref, pad_start_ref[e] + r), sem)

        def tail_copy(j):
            n = zero_scr.shape[0]
            dst = pl.multiple_of((pad_start_ref[N_EXPERTS] + j * tm) * SUBLANES, n)
            return pltpu.make_async_copy(zero_scr, xs_ref.at[pl.ds(dst, n)], sem)

        for e in range(N_EXPERTS):
            lax.fori_loop(0, pad_cnt_ref[e], lambda r, c: (pad_copy(e, r).start(), c)[1], 0)
        lax.fori_loop(0, pad_cnt_ref[N_EXPERTS], lambda j, c: (tail_copy(j).start(), c)[1], 0)
        for e in range(N_EXPERTS):
            lax.fori_loop(0, pad_cnt_ref[e], lambda r, c: (pad_copy(e, r).wait(), c)[1], 0)
        lax.fori_loop(0, pad_cnt_ref[N_EXPERTS], lambda j, c: (tail_copy(j).wait(), c)[1], 0)

    for s in range(nchunk):
        rows_scr[pl.ds(s, tm, stride=nchunk), :] = x_ref[:, s * LANES:(s + 1) * LANES]

    def copy(r, k):
        return pltpu.make_async_copy(_tile_of(rows_scr, r),
                                     _tile_of(xs_ref, pos_ref[0, 0, 2 * r + k]), sem)

    def start(r):
        for k in range(2):
            copy(r, k).start(priority=k)

    def wait(r):
        for k in range(2):
            copy(r, k).wait()

    _for_each_row(tm, start)
    _for_each_row(tm, wait)


def _dispatch(pos, pad_start, pad_cnt, x2d, n_rows, *, tm=TD):
    t, d = x2d.shape
    nchunk = d // LANES
    pos3 = pos.reshape(t // tm, 1, 2 * tm)
    grid_spec = pltpu.PrefetchScalarGridSpec(
        num_scalar_prefetch=2,
        grid=(t // tm,),
        in_specs=[
            pl.BlockSpec((1, 1, 2 * tm), lambda i, ps, pc: (i, 0, 0), memory_space=pltpu.SMEM),
            pl.BlockSpec((tm, d), lambda i, ps, pc: (i, 0)),
        ],
        out_specs=pl.BlockSpec(memory_space=pl.ANY),
        scratch_shapes=[pltpu.VMEM((tm * nchunk, LANES), F32),
                        pltpu.VMEM((tm * nchunk, LANES), F32), pltpu.SemaphoreType.DMA],
    )
    return pl.pallas_call(
        functools.partial(_dispatch_kernel, tm=tm),
        grid_spec=grid_spec,
        out_shape=jax.ShapeDtypeStruct((n_rows * nchunk, LANES), F32),
        compiler_params=_params(1),
        name="dispatch",
    )(pad_start, pad_cnt, pos3, x2d)


def _combine_kernel(pos_ref, x_ref, gate_ref, y_ref, o_ref, buf, sem, *, tm):
    nchunk = x_ref.shape[1] // LANES

    def copy(r, k):
        return pltpu.make_async_copy(_tile_of(y_ref, pos_ref[0, 0, 2 * r + k]),
                                     _tile_of(buf.at[k], r), sem)

    def start(r):
        for k in range(2):
            copy(r, k).start(priority=k)

    def wait(r):
        for k in range(2):
            copy(r, k).wait()

    _for_each_row(tm, start)
    _for_each_row(tm, wait)
    g = gate_ref[...]
    for s in range(nchunk):
        cs = slice(s * LANES, (s + 1) * LANES)
        rows = pl.ds(s, tm, stride=nchunk)
        o_ref[:, cs] = x_ref[:, cs] + g[:, 2:3] * buf[0, rows, :] + g[:, 3:4] * buf[1, rows, :]


def _combine(pos, x2d, info, y, *, tm=TD):
    t, d = x2d.shape
    pos3 = pos.reshape(t // tm, 1, 2 * tm)
    return pl.pallas_call(
        functools.partial(_combine_kernel, tm=tm),
        grid=(t // tm,),
        in_specs=[
            pl.BlockSpec((1, 1, 2 * tm), lambda i: (i, 0, 0), memory_space=pltpu.SMEM),
            pl.BlockSpec((tm, d), lambda i: (i, 0)),
            pl.BlockSpec((tm, LANES), lambda i: (i, 0)),
            pl.BlockSpec(memory_space=pl.ANY),
        ],
        out_specs=pl.BlockSpec((tm, d), lambda i: (i, 0)),
        out_shape=jax.ShapeDtypeStruct((t, d), F32),
        scratch_shapes=[pltpu.VMEM((2, tm * d // LANES, LANES), F32), pltpu.SemaphoreType.DMA],
        compiler_params=_params(1),
        name="combine",
    )(pos3, x2d, info, y)


def _even_layer(x2d, bsz, seqlen, norm1, w_in, conv_qkv, a_log, dt_bias, o_norm, conv_sc, w_out,
                norm2, w1, w3, w2):
    w4 = DN_WIDTH
    c_ab = 4 * w4
    c_sc = c_ab + 2 * DN_HEADS
    w_main = jnp.concatenate([w_in[:, :c_ab], w_in[:, c_sc:]], axis=1).astype(BF16)
    w_ab = jnp.zeros((D_MODEL, LANES), F32).at[:, :2 * DN_HEADS].set(w_in[:, c_ab:c_sc])
    proj, ab = _inproj_even(x2d, norm1, w_main, w_ab, conv_qkv, conv_sc, seqlen)
    x2d = _deltanet(proj.reshape(bsz, seqlen, -1), ab.reshape(bsz, seqlen, LANES),
                    a_log, dt_bias, o_norm, x2d.reshape(bsz, seqlen, -1),
                    w_out.astype(BF16)).reshape(bsz * seqlen, -1)
    n_tiles = x2d.shape[0] // TM_FFN
    te = jnp.zeros((n_tiles,), jnp.int32)
    nused = jnp.full((1,), n_tiles, jnp.int32)
    return _ffn(te, nused, x2d, norm2, w1[None], w3[None], w2[None], add_residual=True)


def _odd_layer(x2d, bsz, seqlen, norm1, w_in, q_norm, k_norm, sinks, w_out, norm2, w_router,
               w1, w3, w2, *, tm=TM_FFN):
    t = x2d.shape[0]
    qkv = _inproj_odd(x2d, norm1, w_in.astype(BF16), q_norm, k_norm)
    x2d = _swa(qkv.reshape(bsz, seqlen, -1), sinks, x2d.reshape(bsz, seqlen, -1),
               w_out.astype(BF16)).reshape(t, -1)

    info, cnt = _router(x2d, norm2, w_router)
    counts = cnt[0, :N_EXPERTS].astype(jnp.int32)
    padded = ((counts + tm - 1) // tm) * tm
    ends = jnp.cumsum(padded)
    offsets = ends - padded
    n_rows = 2 * t + N_EXPERTS * tm
    n_tiles = n_rows // tm
    tile_start = jnp.arange(n_tiles, dtype=jnp.int32) * tm
    tile_expert = jnp.minimum(jnp.sum(tile_start[:, None] >= ends[None, :], axis=1),
                              N_EXPERTS - 1).astype(jnp.int32)
    n_used = (ends[-1:] // tm).astype(jnp.int32)
    idx = info[:, 0:2].astype(jnp.int32)
    pos = (offsets[idx] + info[:, 4:6].astype(jnp.int32)).reshape(-1)

    pad_start = jnp.concatenate([offsets + counts, ends[-1:]]).astype(jnp.int32)
    pad_cnt = jnp.concatenate([padded - counts, (n_rows - ends[-1:]) // TD]).astype(jnp.int32)
    xs = _dispatch(pos, pad_start, pad_cnt, x2d, n_rows)
    ys = _ffn(tile_expert, n_used, xs, norm2, w1, w3, w2, add_residual=False, row_tiles=True,
              tm=tm)
    return _combine(pos, x2d, info, ys)


def kernel(x, ev_norm1, ev_w_in, ev_conv_qkv, ev_a_log, ev_dt_bias, ev_o_norm, ev_conv_sc,
           ev_w_out, ev_norm2, ev_ffn_w1, ev_ffn_w3, ev_ffn_w2, od_norm1, od_w_in, od_q_norm,
           od_k_norm, od_sinks, od_w_out, od_norm2, od_router, od_moe_w1, od_moe_w3, od_moe_w2):
    bsz, seqlen, d = x.shape
    x2d = x.reshape(bsz * seqlen, d)
    depth = ev_norm1.shape[0] + od_norm1.shape[0]
    for layer in range(depth):
        i = layer // 2
        if layer % 2 == 0:
            x2d = _even_layer(x2d, bsz, seqlen, ev_norm1[i], ev_w_in[i], ev_conv_qkv[i],
                              ev_a_log[i], ev_dt_bias[i], ev_o_norm[i], ev_conv_sc[i],
                              ev_w_out[i], ev_norm2[i], ev_ffn_w1[i], ev_ffn_w3[i], ev_ffn_w2[i])
        else:
            x2d = _odd_layer(x2d, bsz, seqlen, od_norm1[i], od_w_in[i], od_q_norm[i],
                             od_k_norm[i], od_sinks[i], od_w_out[i], od_norm2[i], od_router[i],
                             od_moe_w1[i], od_moe_w3[i], od_moe_w2[i])
    return x2d.reshape(bsz, seqlen, d)
```
